```python
import jax, jax.numpy as jnp
from jax import lax
import numpy as np

D_MODEL = 1024
BATCH = 4
SEQ = 8192
DEPTH = 1

D_CONV = D_MODEL
CONV_GROUPS = 8
CONV_A_WIDTH = 3
D_RNN = D_MODEL
RNN_HEADS = 4
RNN_BLOCK = D_RNN // RNN_HEADS
CONV_B_WIDTH = 4
LRU_C = 8.0
D_FF = ((8 * D_MODEL + 3 * 256 - 1) // (3 * 256)) * 256
N_MOD = 6
EPS = 1e-6
IN_WIDTHS = (D_CONV, D_CONV, D_CONV, D_RNN, D_RNN, D_MODEL, D_MODEL)
IN_TOTAL = sum(IN_WIDTHS)
IN_SPLITS = tuple(int(v) for v in np.cumsum(IN_WIDTHS)[:-1])

kernel_name = "hybrid_conv_rglru_gated_merge_adaln"


def rmsnorm(x, g):
    xf = x.astype(jnp.float32)
    y = xf * lax.rsqrt(jnp.mean(xf * xf, axis=-1, keepdims=True) + EPS) * g.astype(jnp.float32)
    return y.astype(x.dtype)


def modulate(h, shift, scale):
    return h * (1.0 + scale[:, None, :]) + shift[:, None, :]


def causal_depthwise_conv(u, w):
    k, ch = w.shape
    return lax.conv_general_dilated(
        u, w[:, None, :].astype(u.dtype), window_strides=(1,), padding=[(k - 1, 0)],
        dimension_numbers=("NWC", "WIO", "NWC"), feature_group_count=ch)


def block_diag_linear(u, w, b):
    bs, s, d = u.shape
    uh = u.reshape(bs, s, RNN_HEADS, RNN_BLOCK)
    return jnp.einsum("bshi,hij->bshj", uh, w).reshape(bs, s, d) + b


def rg_lru(u, w_a, b_a, w_x, b_x, lam):
    r = jax.nn.sigmoid(block_diag_linear(u, w_a, b_a)).astype(jnp.float32)
    i = jax.nn.sigmoid(block_diag_linear(u, w_x, b_x))
    log_a = LRU_C * r * jax.nn.log_sigmoid(lam.astype(jnp.float32))
    a = jnp.exp(log_a)
    mult = jnp.sqrt(jnp.maximum(-jnp.expm1(2.0 * log_a), 0.0))
    mult = mult.at[:, 0].set(1.0)
    bx = mult * (i * u).astype(jnp.float32)

    def combine(left, right):
        a1, b1 = left
        a2, b2 = right
        return a1 * a2, a2 * b1 + b2

    _, h = lax.associative_scan(combine, (a, bx), axis=1)
    return h.astype(u.dtype)


def setup_inputs(seed: int = 0) -> dict:
    key = jax.random.key(seed)
    ks = jax.random.split(key, 20)
    f32 = jnp.float32
    nrm = lambda k, shape, s: jax.random.normal(k, shape, f32) * s
    a8 = jax.random.uniform(ks[11], (DEPTH, D_RNN), f32, 0.9, 0.999)
    base = a8 ** (1.0 / LRU_C)
    lru_lambda = jnp.log(base) - jnp.log1p(-base)
    return {
        "x": nrm(ks[0], (BATCH, SEQ, D_MODEL), 1.0),
        "c": nrm(ks[1], (BATCH, D_MODEL), 1.0),
        "w_ada": nrm(ks[2], (DEPTH, D_MODEL, N_MOD * D_MODEL), 0.5 * D_MODEL ** -0.5),
        "b_ada": nrm(ks[3], (DEPTH, N_MOD * D_MODEL), 0.01),
        "g_norm_mix": 1.0 + nrm(ks[4], (DEPTH, D_MODEL), 0.05),
        "w_in": nrm(ks[5], (DEPTH, D_MODEL, IN_TOTAL), D_MODEL ** -0.5),
        "conv_a_w": nrm(ks[6], (DEPTH, CONV_A_WIDTH, D_CONV), CONV_A_WIDTH ** -0.5),
        "conv_b_w": nrm(ks[7], (DEPTH, CONV_B_WIDTH, D_RNN), CONV_B_WIDTH ** -0.5),
        "conv_b_bias": nrm(ks[8], (DEPTH, D_RNN), 0.01),
        "w_rg_a": nrm(ks[9], (DEPTH, RNN_HEADS, RNN_BLOCK, RNN_BLOCK), RNN_BLOCK ** -0.5),
        "b_rg_a": nrm(ks[10], (DEPTH, D_RNN), 0.01),
        "w_rg_x": nrm(ks[12], (DEPTH, RNN_HEADS, RNN_BLOCK, RNN_BLOCK), RNN_BLOCK ** -0.5),
        "b_rg_x": nrm(ks[13], (DEPTH, D_RNN), 0.01),
        "lru_lambda": lru_lambda,
        "w_out": nrm(ks[14], (DEPTH, D_MODEL, D_MODEL), D_MODEL ** -0.5),
        "g_norm_ffn": 1.0 + nrm(ks[15], (DEPTH, D_MODEL), 0.05),
        "w_gate_up": nrm(ks[16], (DEPTH, D_MODEL, 2 * D_FF), D_MODEL ** -0.5),
        "w_down": nrm(ks[17], (DEPTH, D_FF, D_MODEL), D_FF ** -0.5),
        "g_norm_final": 1.0 + nrm(ks[18], (D_MODEL,), 0.05),
    }


def reference(x, c, w_ada, b_ada, g_norm_mix, w_in, conv_a_w, conv_b_w, conv_b_bias,
              w_rg_a, b_rg_a, w_rg_x, b_rg_x, lru_lambda, w_out, g_norm_ffn,
              w_gate_up, w_down, g_norm_final):
    c_act = jax.nn.silu(c)
    for l in range(DEPTH):
        mod = c_act @ w_ada[l] + b_ada[l]
        sh1, sc1, gt1, sh2, sc2, gt2 = jnp.split(mod, N_MOD, axis=-1)

        h = modulate(rmsnorm(x, g_norm_mix[l]), sh1, sc1)
        proj = h @ w_in[l]
        cb, cc, cx, rx, rg, ga, gb = jnp.split(proj, IN_SPLITS, axis=-1)
        y_a = cb * causal_depthwise_conv(cc * cx, conv_a_w[l])
        u = causal_depthwise_conv(rx, conv_b_w[l]) + conv_b_bias[l]
        y_b = rg_lru(u, w_rg_a[l], b_rg_a[l], w_rg_x[l], b_rg_x[l], lru_lambda[l]) * jax.nn.gelu(rg)
        merged = jax.nn.sigmoid(ga) * y_a + jax.nn.sigmoid(gb) * y_b
        x = x + gt1[:, None, :] * (merged @ w_out[l])

        h = modulate(rmsnorm(x, g_norm_ffn[l]), sh2, sc2)
        g_ff, u_ff = jnp.split(h @ w_gate_up[l], 2, axis=-1)
        x = x + gt2[:, None, :] * ((jax.nn.silu(g_ff) * u_ff) @ w_down[l])
    return rmsnorm(x, g_norm_final)
```

```python
import functools

import jax
import jax.numpy as jnp
from jax import lax
from jax.experimental import pallas as pl
from jax.experimental.pallas import tpu as pltpu

EPS = 1e-6
LRU_C = 8.0
N_MOD = 6
RNN_HEADS = 4
N_PROJ = 7
TIME_BLOCKS = 8
LANES = 128
TM = 256
VMEM_LIMIT_BYTES = 60 * 1024 * 1024

ROW_GMIX, ROW_CA, ROW_CB, ROW_CBIAS, ROW_BA, ROW_BX, ROW_LAM, N_VEC_ROWS = 0, 1, 4, 8, 9, 10, 11, 16


def _shift_down_one_row(blk, carry_row, first_row):
    return jnp.where(first_row, carry_row, pltpu.roll(blk, 1, 0))


def _row_scan(a, b):
    n = a.shape[0]
    rows = lax.broadcasted_iota(jnp.int32, a.shape, 0)
    k = 1
    while k < n:
        keep = rows >= k
        a_sh = jnp.where(keep, pltpu.roll(a, k, 0), 1.0)
        b_sh = jnp.where(keep, pltpu.roll(b, k, 0), 0.0)
        b = a * b_sh + b
        a = a * a_sh
        k *= 2
    return a, b


def _adaln_kernel(c_ref, w_ref, b_ref, o_ref):
    c_act = jax.nn.silu(c_ref[...])
    o_ref[...] = jnp.dot(c_act, w_ref[...], preferred_element_type=jnp.float32,
                         precision=lax.Precision.HIGHEST) + b_ref[...]


def _adaln(c, w_ada, b_ada):
    b, d = c.shape
    n = w_ada.shape[1]
    rows = 8
    c_pad = jnp.zeros((rows, d), jnp.float32).at[:b].set(c)
    bn = d
    out = pl.pallas_call(
        _adaln_kernel,
        grid=(n // bn,),
        in_specs=[pl.BlockSpec((rows, d), lambda i: (0, 0)),
                  pl.BlockSpec((d, bn), lambda i: (0, i)),
                  pl.BlockSpec((1, bn), lambda i: (0, i))],
        out_specs=pl.BlockSpec((rows, bn), lambda i: (0, i)),
        out_shape=jax.ShapeDtypeStruct((rows, n), jnp.float32),
        name="adaln",
    )(c_pad, w_ada, b_ada.reshape(1, n))
    return out[:b].reshape(b, N_MOD, d)


def _mixer_kernel(x_ref, mod_ref, vec_ref, win_ref, wa_ref, wx_ref, wout_ref, o_ref,
                  slab_ref, xp_ref, mrg_ref, hcar_ref, vcar_ref, rcar_ref):
    tm, d = xp_ref.shape
    g_rows = tm // TIME_BLOCKS
    hw = d // RNN_HEADS
    nt = pl.program_id(1)

    for k in range(d // LANES):
        slab_ref[k] = x_ref[0, :, k * LANES:(k + 1) * LANES]
    xp = jnp.concatenate(
        [jnp.concatenate([slab_ref[k, pl.ds(j, g_rows, stride=TIME_BLOCKS), :]
                          for k in range(d // LANES)], axis=1)
         for j in range(TIME_BLOCKS)], axis=0)
    xp_ref[...] = xp

    mod = mod_ref[0]
    sh1, sc1, gt1 = mod[0:1], mod[1:2], mod[2:3]
    ms = jnp.mean(xp * xp, axis=-1, keepdims=True)
    hn = xp * lax.rsqrt(ms + EPS) * vec_ref[ROW_GMIX:ROW_GMIX + 1, :]
    h = (hn * (1.0 + sc1) + sh1).astype(jnp.bfloat16)

    @pl.when(nt == 0)
    def _():
        hcar_ref[...] = jnp.zeros_like(hcar_ref)
        vcar_ref[...] = jnp.zeros_like(vcar_ref)
        rcar_ref[...] = jnp.zeros_like(rcar_ref)

    first_row = lax.broadcasted_iota(jnp.int32, (g_rows, hw), 0) == 0
    seq_start = jnp.logical_and(first_row, nt == 0)

    def blocks(v):
        return [v[j * g_rows:(j + 1) * g_rows] for j in range(TIME_BLOCKS)]

    def conv(blks, carry_ref, w_rows, cs):
        kw = len(w_rows)
        shifted = {}
        for m in range(TIME_BLOCKS - (kw - 1), TIME_BLOCKS):
            r = m - (TIME_BLOCKS - (kw - 1))
            shifted[m] = _shift_down_one_row(blks[m], carry_ref[r:r + 1, cs], first_row)
        for m in range(TIME_BLOCKS - (kw - 1), TIME_BLOCKS):
            r = m - (TIME_BLOCKS - (kw - 1))
            carry_ref[r:r + 1, cs] = blks[m][g_rows - 1:g_rows]
        out = []
        for j in range(TIME_BLOCKS):
            acc = None
            for k in range(kw):
                back = kw - 1 - k
                src = blks[j - back] if j >= back else shifted[j - back + TIME_BLOCKS]
                term = w_rows[k] * src
                acc = term if acc is None else acc + term
            out.append(acc)
        return out

    for hd in range(RNN_HEADS):
        cs = slice(hd * hw, (hd + 1) * hw)
        proj = jnp.dot(h, win_ref[:, hd * N_PROJ * hw:(hd + 1) * N_PROJ * hw],
                       preferred_element_type=jnp.float32)
        cb, cc, cx, rx, rg, ga, gb = [proj[:, i * hw:(i + 1) * hw] for i in range(N_PROJ)]

        wca = [vec_ref[ROW_CA + k:ROW_CA + k + 1, cs] for k in range(3)]
        conv_a = conv(blocks(cc * cx), vcar_ref, wca, cs)
        y_a = [b_j * c_j for b_j, c_j in zip(blocks(cb), conv_a)]

        wcb = [vec_ref[ROW_CB + k:ROW_CB + k + 1, cs] for k in range(4)]
        bias = vec_ref[ROW_CBIAS:ROW_CBIAS + 1, cs]
        u_blk = [u_j + bias for u_j in conv(blocks(rx), rcar_ref, wcb, cs)]
        u = jnp.concatenate(u_blk, axis=0)
        u16 = u.astype(jnp.bfloat16)
        r = jax.nn.sigmoid(jnp.dot(u16, wa_ref[hd], preferred_element_type=jnp.float32)
                           + vec_ref[ROW_BA:ROW_BA + 1, cs])
        i = jax.nn.sigmoid(jnp.dot(u16, wx_ref[hd], preferred_element_type=jnp.float32)
                           + vec_ref[ROW_BX:ROW_BX + 1, cs])
        log_a = LRU_C * r * jax.nn.log_sigmoid(vec_ref[ROW_LAM:ROW_LAM + 1, cs])
        a = jnp.exp(log_a)
        mult = jnp.sqrt(jnp.maximum(1.0 - a * a, 0.0))
        a_blk = blocks(a)
        m_blk = blocks(mult)
        m_blk[0] = jnp.where(seq_start, 1.0, m_blk[0])
        iu_blk = blocks(i * u)
        bx_blk = [m_j * iu_j for m_j, iu_j in zip(m_blk, iu_blk)]

        a_run, b_run = a_blk[0], bx_blk[0]
        for j in range(1, TIME_BLOCKS):
            b_run = a_blk[j] * b_run + bx_blk[j]
            a_run = a_blk[j] * a_run
        a_grp, b_grp = _row_scan(a_run, b_run)
        h0 = hcar_ref[0:1, cs]
        h_end = b_grp + a_grp * h0
        h_prev = _shift_down_one_row(h_end, h0, first_row)
        hcar_ref[0:1, cs] = h_end[g_rows - 1:g_rows]

        gelu_blk = blocks(jax.nn.gelu(rg))
        sa_blk = blocks(jax.nn.sigmoid(ga))
        sb_blk = blocks(jax.nn.sigmoid(gb))
        for j in range(TIME_BLOCKS):
            h_prev = a_blk[j] * h_prev + bx_blk[j]
            y_b = h_prev * gelu_blk[j]
            merged = sa_blk[j] * y_a[j] + sb_blk[j] * y_b
            mrg_ref[j * g_rows:(j + 1) * g_rows, cs] = merged.astype(jnp.bfloat16)

    out = jnp.dot(mrg_ref[...], wout_ref[...], preferred_element_type=jnp.float32)
    o_ref[0] = xp_ref[...] + gt1 * out


def _mixer(x, mod, vecs, w_in_heads, w_rg_a, w_rg_x, w_out):
    b, s, d = x.shape
    hw = d // RNN_HEADS
    const = dict(pipeline_mode=pl.Buffered(1))
    return pl.pallas_call(
        _mixer_kernel,
        grid=(b, s // TM),
        in_specs=[pl.BlockSpec((1, TM, d), lambda bi, ti: (bi, ti, 0)),
                  pl.BlockSpec((1, N_MOD, d), lambda bi, ti: (bi, 0, 0)),
                  pl.BlockSpec((N_VEC_ROWS, d), lambda bi, ti: (0, 0), **const),
                  pl.BlockSpec(w_in_heads.shape, lambda bi, ti: (0, 0), **const),
                  pl.BlockSpec((RNN_HEADS, hw, hw), lambda bi, ti: (0, 0, 0), **const),
                  pl.BlockSpec((RNN_HEADS, hw, hw), lambda bi, ti: (0, 0, 0), **const),
                  pl.BlockSpec((d, d), lambda bi, ti: (0, 0), **const)],
        out_specs=pl.BlockSpec((1, TM, d), lambda bi, ti: (bi, ti, 0)),
        out_shape=jax.ShapeDtypeStruct((b, s, d), jnp.float32),
        scratch_shapes=[pltpu.VMEM((d // LANES, TM, LANES), jnp.float32),
                        pltpu.VMEM((TM, d), jnp.float32),
                        pltpu.VMEM((TM, d), jnp.bfloat16),
                        pltpu.VMEM((8, d), jnp.float32),
                        pltpu.VMEM((8, d), jnp.float32),
                        pltpu.VMEM((8, d), jnp.float32)],
        compiler_params=pltpu.CompilerParams(
            dimension_semantics=("arbitrary", "arbitrary"),
            vmem_limit_bytes=VMEM_LIMIT_BYTES),
        name="mixer",
    )(x, mod, vecs, w_in_heads, w_rg_a, w_rg_x, w_out)


def _ffn_kernel(x_ref, mod_ref, gn_ref, wg_ref, wu_ref, wd_ref, o_ref, slab_ref, *, final_norm):
    tm, d = x_ref.shape[1], x_ref.shape[2]
    g_rows = tm // TIME_BLOCKS
    x1 = x_ref[0]
    mod = mod_ref[0]
    sh2, sc2, gt2 = mod[3:4], mod[4:5], mod[5:6]
    ms = jnp.mean(x1 * x1, axis=-1, keepdims=True)
    hn = x1 * lax.rsqrt(ms + EPS) * gn_ref[0:1, :]
    h = (hn * (1.0 + sc2) + sh2).astype(jnp.bfloat16)
    g_ff = jnp.dot(h, wg_ref[...], preferred_element_type=jnp.float32)
    u_ff = jnp.dot(h, wu_ref[...], preferred_element_type=jnp.float32)
    act = (jax.nn.silu(g_ff) * u_ff).astype(jnp.bfloat16)
    x2 = x1 + gt2 * jnp.dot(act, wd_ref[...], preferred_element_type=jnp.float32)
    if final_norm:
        ms2 = jnp.mean(x2 * x2, axis=-1, keepdims=True)
        x2 = x2 * lax.rsqrt(ms2 + EPS) * gn_ref[1:2, :]
    for k in range(d // LANES):
        for j in range(TIME_BLOCKS):
            slab_ref[k, pl.ds(j, g_rows, stride=TIME_BLOCKS), :] = (
                x2[j * g_rows:(j + 1) * g_rows, k * LANES:(k + 1) * LANES])
        o_ref[0, :, k * LANES:(k + 1) * LANES] = slab_ref[k]


def _ffn(x1, mod, gnorms, w_gate, w_up, w_down, final_norm):
    b, s, d = x1.shape
    dff = w_gate.shape[1]
    const = dict(pipeline_mode=pl.Buffered(1))
    return pl.pallas_call(
        functools.partial(_ffn_kernel, final_norm=final_norm),
        grid=(b, s // TM),
        in_specs=[pl.BlockSpec((1, TM, d), lambda bi, ti: (bi, ti, 0)),
                  pl.BlockSpec((1, N_MOD, d), lambda bi, ti: (bi, 0, 0)),
                  pl.BlockSpec((8, d), lambda bi, ti: (0, 0), **const),
                  pl.BlockSpec((d, dff), lambda bi, ti: (0, 0), **const),
                  pl.BlockSpec((d, dff), lambda bi, ti: (0, 0), **const),
                  pl.BlockSpec((dff, d), lambda bi, ti: (0, 0), **const)],
        out_specs=pl.BlockSpec((1, TM, d), lambda bi, ti: (bi, ti, 0)),
        out_shape=jax.ShapeDtypeStruct((b, s, d), jnp.float32),
        scratch_shapes=[pltpu.VMEM((d // LANES, TM, LANES), jnp.float32)],
        compiler_params=pltpu.CompilerParams(
            dimension_semantics=("arbitrary", "arbitrary"),
            vmem_limit_bytes=VMEM_LIMIT_BYTES),
        name="ffn",
    )(x1, mod, gnorms, w_gate, w_up, w_down)


def kernel(x, c, w_ada, b_ada, g_norm_mix, w_in, conv_a_w, conv_b_w, conv_b_bias,
           w_rg_a, b_rg_a, w_rg_x, b_rg_x, lru_lambda, w_out, g_norm_ffn,
           w_gate_up, w_down, g_norm_final):
    depth = w_ada.shape[0]
    b, s, d = x.shape
    hw = d // RNN_HEADS
    dff = w_down.shape[1]
    assert s % TM == 0 and TM % (8 * TIME_BLOCKS) == 0 and d % LANES == 0
    assert w_in.shape[2] == N_PROJ * d and w_rg_a.shape[1:] == (RNN_HEADS, hw, hw)
    bf16 = jnp.bfloat16
    for l in range(depth):
        mod = _adaln(c, w_ada[l], b_ada[l])
        w_in_heads = (w_in[l].reshape(d, N_PROJ, RNN_HEADS, hw).transpose(0, 2, 1, 3)
                      .reshape(d, N_PROJ * d).astype(bf16))
        vecs = jnp.concatenate(
            [g_norm_mix[l][None], conv_a_w[l], conv_b_w[l], conv_b_bias[l][None],
             b_rg_a[l][None], b_rg_x[l][None], lru_lambda[l][None],
             jnp.zeros((N_VEC_ROWS - ROW_LAM - 1, d), jnp.float32)], axis=0)
        x1 = _mixer(x, mod, vecs, w_in_heads, w_rg_a[l].astype(bf16), w_rg_x[l].astype(bf16),
                    w_out[l].astype(bf16))
        gnorms = jnp.concatenate([g_norm_ffn[l][None], g_norm_final[None],
                                  jnp.zeros((6, d), jnp.float32)], axis=0)
        x = _ffn(x1, mod, gnorms, w_gate_up[l][:, :dff].astype(bf16),
                 w_gate_up[l][:, dff:].astype(bf16), w_down[l].astype(bf16),
                 final_norm=(l == depth - 1))
    return x
```

```python
import functools

import jax
import jax.numpy as jnp
from jax import lax
from jax.experimental import pallas as pl
from jax.experimental.pallas import tpu as pltpu

EPS = 1e-6
LRU_C = 8.0
N_MOD = 6
RNN_HEADS = 4
N_PROJ = 7
TIME_BLOCKS = 8
LANES = 128
MXU_TILE = 256
TM = 256
VMEM_LIMIT_BYTES = 60 * 1024 * 1024

(ROW_GMIX, ROW_CA, ROW_CB, ROW_CBIAS, ROW_BA, ROW_BX, ROW_LAM, ROW_GFFN, ROW_GFINAL,
 N_VEC_ROWS) = 0, 1, 4, 8, 9, 10, 11, 12, 13, 16


def _shift_down_one_row(blk, carry_row, first_row):
    return jnp.where(first_row, carry_row, pltpu.roll(blk, 1, 0))


def _row_scan(a, b):
    n = a.shape[0]
    rows = lax.broadcasted_iota(jnp.int32, a.shape, 0)
    k = 1
    while k < n:
        keep = rows >= k
        a_sh = jnp.where(keep, pltpu.roll(a, k, 0), 1.0)
        b_sh = jnp.where(keep, pltpu.roll(b, k, 0), 0.0)
        b = a * b_sh + b
        a = a * a_sh
        k *= 2
    return a, b


def _adaln_kernel(c_ref, w_ref, b_ref, o_ref):
    c_act = jax.nn.silu(c_ref[...])
    o_ref[...] = jnp.dot(c_act, w_ref[...], preferred_element_type=jnp.float32,
                         precision=lax.Precision.HIGHEST) + b_ref[...]


def _adaln(c, w_ada, b_ada):
    b, d = c.shape
    n = w_ada.shape[1]
    rows = 8
    c_pad = jnp.zeros((rows, d), jnp.float32).at[:b].set(c)
    bn = d
    out = pl.pallas_call(
        _adaln_kernel,
        grid=(n // bn,),
        in_specs=[pl.BlockSpec((rows, d), lambda i: (0, 0)),
                  pl.BlockSpec((d, bn), lambda i: (0, i)),
                  pl.BlockSpec((1, bn), lambda i: (0, i))],
        out_specs=pl.BlockSpec((rows, bn), lambda i: (0, i)),
        out_shape=jax.ShapeDtypeStruct((rows, n), jnp.float32),
        name="adaln",
    )(c_pad, w_ada, b_ada.reshape(1, n))
    return out[:b].reshape(b, N_MOD, d)


def _rms_modulate(x, gain, shift, scale):
    ms = jnp.mean(x * x, axis=-1, keepdims=True)
    return (x * lax.rsqrt(ms + EPS) * gain) * (1.0 + scale) + shift


def _ffn_chunks(dff):
    tiles = dff // MXU_TILE
    bounds = [MXU_TILE * ((tiles * i + RNN_HEADS - 1) // RNN_HEADS) for i in range(RNN_HEADS + 1)]
    return list(zip(bounds[:-1], bounds[1:]))


def _mixer_head(hd, h, seq_start_tile, vec_ref, win_ref, wa_ref, wx_ref,
                mrg_ref, hcar_ref, vcar_ref, rcar_ref):
    tm, d = h.shape
    g_rows = tm // TIME_BLOCKS
    hw = d // RNN_HEADS
    cs = slice(hd * hw, (hd + 1) * hw)
    first_row = lax.broadcasted_iota(jnp.int32, (g_rows, hw), 0) == 0
    seq_start = jnp.logical_and(first_row, seq_start_tile)

    def blocks(v):
        return [v[j * g_rows:(j + 1) * g_rows] for j in range(TIME_BLOCKS)]

    def conv(blks, carry_ref, w_rows):
        kw = len(w_rows)
        tail = range(TIME_BLOCKS - (kw - 1), TIME_BLOCKS)
        shifted = {m: _shift_down_one_row(blks[m], carry_ref[r:r + 1, cs], first_row)
                   for r, m in enumerate(tail)}
        for r, m in enumerate(tail):
            carry_ref[r:r + 1, cs] = blks[m][g_rows - 1:g_rows]
        out = []
        for j in range(TIME_BLOCKS):
            acc = None
            for k in range(kw):
                back = kw - 1 - k
                src = blks[j - back] if j >= back else shifted[j - back + TIME_BLOCKS]
                term = w_rows[k] * src
                acc = term if acc is None else acc + term
            out.append(acc)
        return out

    proj = jnp.dot(h, win_ref[:, hd * N_PROJ * hw:(hd + 1) * N_PROJ * hw],
                   preferred_element_type=jnp.float32)
    cb, cc, cx, rx, rg, ga, gb = [proj[:, i * hw:(i + 1) * hw] for i in range(N_PROJ)]

    wca = [vec_ref[ROW_CA + k:ROW_CA + k + 1, cs] for k in range(3)]
    conv_a = conv(blocks(cc * cx), vcar_ref, wca)
    y_a = [b_j * c_j for b_j, c_j in zip(blocks(cb), conv_a)]

    wcb = [vec_ref[ROW_CB + k:ROW_CB + k + 1, cs] for k in range(4)]
    bias = vec_ref[ROW_CBIAS:ROW_CBIAS + 1, cs]
    u_blk = [u_j + bias for u_j in conv(blocks(rx), rcar_ref, wcb)]
    u = jnp.concatenate(u_blk, axis=0)
    u16 = u.astype(jnp.bfloat16)
    r = jax.nn.sigmoid(jnp.dot(u16, wa_ref[hd], preferred_element_type=jnp.float32)
                       + vec_ref[ROW_BA:ROW_BA + 1, cs])
    i = jax.nn.sigmoid(jnp.dot(u16, wx_ref[hd], preferred_element_type=jnp.float32)
                       + vec_ref[ROW_BX:ROW_BX + 1, cs])
    log_a = LRU_C * r * jax.nn.log_sigmoid(vec_ref[ROW_LAM:ROW_LAM + 1, cs])
    a = jnp.exp(log_a)
    mult = jnp.sqrt(jnp.maximum(1.0 - a * a, 0.0))
    a_blk = blocks(a)
    m_blk = blocks(mult)
    m_blk[0] = jnp.where(seq_start, 1.0, m_blk[0])
    iu_blk = blocks(i * u)
    bx_blk = [m_j * iu_j for m_j, iu_j in zip(m_blk, iu_blk)]

    a_run, b_run = a_blk[0], bx_blk[0]
    for j in range(1, TIME_BLOCKS):
        b_run = a_blk[j] * b_run + bx_blk[j]
        a_run = a_blk[j] * a_run
    a_grp, b_grp = _row_scan(a_run, b_run)
    h0 = hcar_ref[0:1, cs]
    h_end = b_grp + a_grp * h0
    h_prev = _shift_down_one_row(h_end, h0, first_row)
    hcar_ref[0:1, cs] = h_end[g_rows - 1:g_rows]

    gelu_blk = blocks(jax.nn.gelu(rg))
    sa_blk = blocks(jax.nn.sigmoid(ga))
    sb_blk = blocks(jax.nn.sigmoid(gb))
    for j in range(TIME_BLOCKS):
        h_prev = a_blk[j] * h_prev + bx_blk[j]
        y_b = h_prev * gelu_blk[j]
        merged = sa_blk[j] * y_a[j] + sb_blk[j] * y_b
        mrg_ref[j * g_rows:(j + 1) * g_rows, cs] = merged.astype(jnp.bfloat16)


def _ffn_chunk(h, lo, hi, wgu_ref, wd_ref):
    dff = wd_ref.shape[0]
    g_ff = jnp.dot(h, wgu_ref[:, lo:hi], preferred_element_type=jnp.float32)
    u_ff = jnp.dot(h, wgu_ref[:, dff + lo:dff + hi], preferred_element_type=jnp.float32)
    act = (jax.nn.silu(g_ff) * u_ff).astype(jnp.bfloat16)
    return jnp.dot(act, wd_ref[lo:hi, :], preferred_element_type=jnp.float32)


def _layer_kernel(x_ref, modm_ref, modf_ref, vec_ref, win_ref, wa_ref, wx_ref, wout_ref,
                  wgu_ref, wd_ref, o_ref,
                  slab_in_ref, slab_out_ref, xp_ref, x1_ref, mrg_ref, hcar_ref, vcar_ref, rcar_ref,
                  *, tiles_per_seq, final_norm):
    tm, d = x_ref.shape[1], x_ref.shape[2]
    g_rows = tm // TIME_BLOCKS
    s = pl.program_id(0)
    nt = s % tiles_per_seq
    slot = s % 2

    @pl.when(s == 0)
    def _():
        x1_ref[1] = jnp.zeros(x1_ref.shape[1:], x1_ref.dtype)

    @pl.when(nt == 0)
    def _():
        hcar_ref[...] = jnp.zeros_like(hcar_ref)
        vcar_ref[...] = jnp.zeros_like(vcar_ref)
        rcar_ref[...] = jnp.zeros_like(rcar_ref)

    for k in range(d // LANES):
        slab_in_ref[k] = x_ref[0, :, k * LANES:(k + 1) * LANES]
    xp = jnp.concatenate(
        [jnp.concatenate([slab_in_ref[k, pl.ds(j, g_rows, stride=TIME_BLOCKS), :]
                          for k in range(d // LANES)], axis=1)
         for j in range(TIME_BLOCKS)], axis=0)
    xp_ref[...] = xp
    modm = modm_ref[0]
    modf = modf_ref[0]
    h_mix = _rms_modulate(xp, vec_ref[ROW_GMIX:ROW_GMIX + 1, :],
                          modm[0:1], modm[1:2]).astype(jnp.bfloat16)
    h_ffn = _rms_modulate(x1_ref[1 - slot], vec_ref[ROW_GFFN:ROW_GFFN + 1, :],
                          modf[3:4], modf[4:5]).astype(jnp.bfloat16)

    ffn_out = None
    for hd, (lo, hi) in enumerate(_ffn_chunks(wd_ref.shape[0])):
        _mixer_head(hd, h_mix, nt == 0, vec_ref, win_ref, wa_ref, wx_ref,
                    mrg_ref, hcar_ref, vcar_ref, rcar_ref)
        part = _ffn_chunk(h_ffn, lo, hi, wgu_ref, wd_ref)
        ffn_out = part if ffn_out is None else ffn_out + part
    mix_out = jnp.dot(mrg_ref[...], wout_ref[...], preferred_element_type=jnp.float32)

    x2 = x1_ref[1 - slot] + modf[5:6] * ffn_out
    if final_norm:
        ms2 = jnp.mean(x2 * x2, axis=-1, keepdims=True)
        x2 = x2 * lax.rsqrt(ms2 + EPS) * vec_ref[ROW_GFINAL:ROW_GFINAL + 1, :]
    for k in range(d // LANES):
        for j in range(TIME_BLOCKS):
            slab_out_ref[k, pl.ds(j, g_rows, stride=TIME_BLOCKS), :] = (
                x2[j * g_rows:(j + 1) * g_rows, k * LANES:(k + 1) * LANES])
        o_ref[0, :, k * LANES:(k + 1) * LANES] = slab_out_ref[k]

    x1_ref[slot] = xp_ref[...] + modm[2:3] * mix_out


def _layer(x, mod, vecs, w_in_heads, w_rg_a, w_rg_x, w_out, w_gate_up, w_down, final_norm):
    b, s, d = x.shape
    nts = s // TM
    n_tiles = b * nts
    const = dict(pipeline_mode=pl.Buffered(1))

    def mixer_tile(i):
        t = jnp.minimum(i, n_tiles - 1)
        return t // nts, t % nts

    def ffn_tile(i):
        t = jnp.maximum(i - 1, 0)
        return t // nts, t % nts

    def whole(arr):
        return pl.BlockSpec(arr.shape, lambda i: (0,) * arr.ndim, **const)

    return pl.pallas_call(
        functools.partial(_layer_kernel, tiles_per_seq=nts, final_norm=final_norm),
        grid=(n_tiles + 1,),
        in_specs=[pl.BlockSpec((1, TM, d), lambda i: (*mixer_tile(i), 0)),
                  pl.BlockSpec((1, N_MOD, d), lambda i: (mixer_tile(i)[0], 0, 0)),
                  pl.BlockSpec((1, N_MOD, d), lambda i: (ffn_tile(i)[0], 0, 0)),
                  whole(vecs), whole(w_in_heads), whole(w_rg_a), whole(w_rg_x), whole(w_out),
                  whole(w_gate_up), whole(w_down)],
        out_specs=pl.BlockSpec((1, TM, d), lambda i: (*ffn_tile(i), 0)),
        out_shape=jax.ShapeDtypeStruct((b, s, d), jnp.float32),
        scratch_shapes=[pltpu.VMEM((d // LANES, TM, LANES), jnp.float32),
                        pltpu.VMEM((d // LANES, TM, LANES), jnp.float32),
                        pltpu.VMEM((TM, d), jnp.float32),
                        pltpu.VMEM((2, TM, d), jnp.float32),
                        pltpu.VMEM((TM, d), jnp.bfloat16),
                        pltpu.VMEM((8, d), jnp.float32),
                        pltpu.VMEM((8, d), jnp.float32),
                        pltpu.VMEM((8, d), jnp.float32)],
        compiler_params=pltpu.CompilerParams(
            dimension_semantics=("arbitrary",),
            vmem_limit_bytes=VMEM_LIMIT_BYTES),
        name="layer",
    )(x, mod, mod, vecs, w_in_heads, w_rg_a, w_rg_x, w_out, w_gate_up, w_down)


def kernel(x, c, w_ada, b_ada, g_norm_mix, w_in, conv_a_w, conv_b_w, conv_b_bias,
           w_rg_a, b_rg_a, w_rg_x, b_rg_x, lru_lambda, w_out, g_norm_ffn,
           w_gate_up, w_down, g_norm_final):
    depth = w_ada.shape[0]
    b, s, d = x.shape
    hw = d // RNN_HEADS
    assert s % TM == 0 and TM % (8 * TIME_BLOCKS) == 0 and d % LANES == 0
    assert w_in.shape[2] == N_PROJ * d and w_rg_a.shape[1:] == (RNN_HEADS, hw, hw)
    assert w_down.shape[1] % MXU_TILE == 0
    bf16 = jnp.bfloat16
    for l in range(depth):
        mod = _adaln(c, w_ada[l], b_ada[l])
        w_in_heads = (w_in[l].reshape(d, N_PROJ, RNN_HEADS, hw).transpose(0, 2, 1, 3)
                      .reshape(d, N_PROJ * d).astype(bf16))
        vecs = jnp.concatenate(
            [g_norm_mix[l][None], conv_a_w[l], conv_b_w[l], conv_b_bias[l][None],
             b_rg_a[l][None], b_rg_x[l][None], lru_lambda[l][None], g_norm_ffn[l][None],
             g_norm_final[None], jnp.zeros((N_VEC_ROWS - ROW_GFINAL - 1, d), jnp.float32)],
            axis=0)
        x = _layer(x, mod, vecs, w_in_heads, w_rg_a[l].astype(bf16), w_rg_x[l].astype(bf16),
                   w_out[l].astype(bf16), w_gate_up[l].astype(bf16), w_down[l].astype(bf16),
                   final_norm=(l == depth - 1))
    return x
```

```python
import functools

import jax
import jax.numpy as jnp
from jax import lax
from jax.experimental import pallas as pl
from jax.experimental.pallas import tpu as pltpu

EPS = 1e-6
LRU_C = 8.0
N_MOD = 6
RNN_HEADS = 4
N_PROJ = 7
P_CB, P_CC, P_CX, P_RX, P_RG, P_GA, P_GB = range(N_PROJ)
TIME_BLOCKS = 8
LANES = 128
MXU_TILE = 256
ROWS = 16
TM = 256
VMEM_LIMIT_BYTES = 60 * 1024 * 1024

(ROW_GMIX, ROW_CA, ROW_CB, ROW_CBIAS, ROW_BA, ROW_BX, ROW_LAM, ROW_GFFN, ROW_GFINAL,
 N_VEC_ROWS) = 0, 1, 4, 8, 9, 10, 11, 12, 13, 16


def _sigmoid(x):
    return 0.5 * jnp.tanh(0.5 * x) + 0.5


def _shift_down_one_row(blk, carry_row, first_row):
    return jnp.where(first_row, carry_row, pltpu.roll(blk, 1, 0))


def _row_scan(a, b):
    n = a.shape[0]
    rows = lax.broadcasted_iota(jnp.int32, a.shape, 0)
    k = 1
    while k < n:
        keep = rows >= k
        a_sh = jnp.where(keep, pltpu.roll(a, k, 0), 1.0)
        b_sh = jnp.where(keep, pltpu.roll(b, k, 0), 0.0)
        b = a * b_sh + b
        a = a * a_sh
        k *= 2
    return a, b


def _adaln_kernel(c_ref, w_ref, b_ref, o_ref):
    c_act = jax.nn.silu(c_ref[...]).astype(jnp.bfloat16)
    o_ref[...] = jnp.dot(c_act, w_ref[...].astype(jnp.bfloat16),
                         preferred_element_type=jnp.float32) + b_ref[...]


def _adaln(c, w_ada, b_ada):
    b, d = c.shape
    n = w_ada.shape[1]
    rows = 8
    c_pad = jnp.zeros((rows, d), jnp.float32).at[:b].set(c)
    bn = d
    out = pl.pallas_call(
        _adaln_kernel,
        grid=(n // bn,),
        in_specs=[pl.BlockSpec((rows, d), lambda i: (0, 0)),
                  pl.BlockSpec((d, bn), lambda i: (0, i)),
                  pl.BlockSpec((1, bn), lambda i: (0, i))],
        out_specs=pl.BlockSpec((rows, bn), lambda i: (0, i)),
        out_shape=jax.ShapeDtypeStruct((rows, n), jnp.float32),
        name="adaln",
    )(c_pad, w_ada, b_ada.reshape(1, n))
    return out[:b].reshape(b, N_MOD, d)


def _rms_modulate(x, gain, shift, scale):
    ms = jnp.mean(x * x, axis=-1, keepdims=True)
    return (x * lax.rsqrt(ms + EPS) * gain) * (1.0 + scale) + shift


def _ffn_chunks(dff):
    tiles = dff // MXU_TILE
    bounds = [MXU_TILE * ((tiles * i + RNN_HEADS - 1) // RNN_HEADS) for i in range(RNN_HEADS + 1)]
    return list(zip(bounds[:-1], bounds[1:]))


def _head_proj(hd, h, win_ref):
    d = h.shape[1]
    hw = d // RNN_HEADS
    return jnp.concatenate(
        [jnp.dot(h, win_ref[:, i * d + hd * hw:i * d + (hd + 1) * hw],
                 preferred_element_type=jnp.float32) for i in range(N_PROJ)], axis=1)


def _mixer_head(hd, proj, seq_start_tile, vec_ref, wa_ref, wx_ref,
                ya_ref, u_ref, a_ref, bx_ref, mrg_ref, hcar_ref, vcar_ref, rcar_ref):
    tm = proj.shape[0]
    g_rows = tm // TIME_BLOCKS
    hw = proj.shape[1] // N_PROJ
    n_gc = g_rows // ROWS
    cs = slice(hd * hw, (hd + 1) * hw)
    first_row = lax.broadcasted_iota(jnp.int32, (g_rows, hw), 0) == 0
    seq_start = jnp.logical_and(lax.broadcasted_iota(jnp.int32, (ROWS, hw), 0) == 0,
                                seq_start_tile)

    def block(i, j):
        return proj[j * g_rows:(j + 1) * g_rows, i * hw:(i + 1) * hw]

    def rows(j, gc):
        return slice(j * g_rows + gc * ROWS, j * g_rows + (gc + 1) * ROWS)

    def item(i, j, gc):
        return proj[rows(j, gc), i * hw:(i + 1) * hw]

    def wrapped(blks, carry_ref):
        out = [_shift_down_one_row(b, carry_ref[r:r + 1, cs], first_row)
               for r, b in enumerate(blks)]
        for r, b in enumerate(blks):
            carry_ref[r:r + 1, cs] = b[g_rows - 1:g_rows]
        return out

    def vec(row):
        return vec_ref[row:row + 1, cs]

    wca = [vec(ROW_CA + k) for k in range(3)]
    wcb = [vec(ROW_CB + k) for k in range(4)]
    bias = vec(ROW_CBIAS)
    v_wrap = wrapped([block(P_CC, j) * block(P_CX, j) for j in (6, 7)], vcar_ref)
    r_wrap = wrapped([block(P_RX, j) for j in (5, 6, 7)], rcar_ref)
    for gc in range(n_gc):
        gsl = slice(gc * ROWS, (gc + 1) * ROWS)
        v_hist = [w[gsl] for w in v_wrap]
        r_hist = [w[gsl] for w in r_wrap]
        for j in range(TIME_BLOCKS):
            v = item(P_CC, j, gc) * item(P_CX, j, gc)
            conv_a = wca[0] * v_hist[0] + wca[1] * v_hist[1] + wca[2] * v
            ya_ref[rows(j, gc), cs] = item(P_CB, j, gc) * conv_a
            v_hist = [v_hist[1], v]
            rx = item(P_RX, j, gc)
            u_ref[rows(j, gc), cs] = (wcb[0] * r_hist[0] + wcb[1] * r_hist[1]
                                      + wcb[2] * r_hist[2] + wcb[3] * rx + bias)
            r_hist = [r_hist[1], r_hist[2], rx]

    u16 = u_ref[:, cs].astype(jnp.bfloat16)
    r_pre = jnp.dot(u16, wa_ref[hd], preferred_element_type=jnp.float32)
    i_pre = jnp.dot(u16, wx_ref[hd], preferred_element_type=jnp.float32)
    b_a, b_x = vec(ROW_BA), vec(ROW_BX)
    log_a_scale = LRU_C * jax.nn.log_sigmoid(vec(ROW_LAM))

    a_runs, b_runs = [], []
    for gc in range(n_gc):
        for j in range(TIME_BLOCKS):
            rw = rows(j, gc)
            r = _sigmoid(r_pre[rw] + b_a)
            i = _sigmoid(i_pre[rw] + b_x)
            a = jnp.exp(r * log_a_scale)
            mult = jnp.sqrt(jnp.maximum(1.0 - a * a, 0.0))
            if j == 0 and gc == 0:
                mult = jnp.where(seq_start, 1.0, mult)
            bx = mult * (i * u_ref[rw, cs])
            a_ref[rw, cs] = a
            bx_ref[rw, cs] = bx
            if j == 0:
                a_run, b_run = a, bx
            else:
                b_run = a * b_run + bx
                a_run = a * a_run
        a_runs.append(a_run)
        b_runs.append(b_run)
    a_grp, b_grp = _row_scan(jnp.concatenate(a_runs, axis=0), jnp.concatenate(b_runs, axis=0))
    h0 = hcar_ref[0:1, cs]
    h_end = b_grp + a_grp * h0
    h_in = _shift_down_one_row(h_end, h0, first_row)
    hcar_ref[0:1, cs] = h_end[g_rows - 1:g_rows]

    for gc in range(n_gc):
        h = h_in[gc * ROWS:(gc + 1) * ROWS]
        for j in range(TIME_BLOCKS):
            rw = rows(j, gc)
            h = a_ref[rw, cs] * h + bx_ref[rw, cs]
            y_b = h * jax.nn.gelu(item(P_RG, j, gc))
            merged = (_sigmoid(item(P_GA, j, gc)) * ya_ref[rw, cs]
                      + _sigmoid(item(P_GB, j, gc)) * y_b)
            mrg_ref[rw, cs] = merged.astype(jnp.bfloat16)


def _ffn_gate_up(h, lo, hi, wgu_ref, dff):
    g_ff = jnp.dot(h, wgu_ref[:, lo:hi], preferred_element_type=jnp.float32)
    u_ff = jnp.dot(h, wgu_ref[:, dff + lo:dff + hi], preferred_element_type=jnp.float32)
    return g_ff, u_ff


def _ffn_down(g_ff, u_ff, lo, hi, wd_ref):
    act = (g_ff * _sigmoid(g_ff) * u_ff).astype(jnp.bfloat16)
    return jnp.dot(act, wd_ref[lo:hi, :], preferred_element_type=jnp.float32)


def _load_permuted(x_ref, slab_ref):
    tm, d = x_ref.shape[1], x_ref.shape[2]
    g_rows = tm // TIME_BLOCKS
    for k in range(d // LANES):
        slab_ref[k] = x_ref[0, :, k * LANES:(k + 1) * LANES]
    return jnp.concatenate(
        [jnp.concatenate([slab_ref[k, pl.ds(j, g_rows, stride=TIME_BLOCKS), :]
                          for k in range(d // LANES)], axis=1)
         for j in range(TIME_BLOCKS)], axis=0)


def _store_time_order(x2, o_ref, slab_ref):
    tm, d = x2.shape
    g_rows = tm // TIME_BLOCKS
    for k in range(d // LANES):
        for j in range(TIME_BLOCKS):
            slab_ref[k, pl.ds(j, g_rows, stride=TIME_BLOCKS), :] = (
                x2[j * g_rows:(j + 1) * g_rows, k * LANES:(k + 1) * LANES])
        o_ref[0, :, k * LANES:(k + 1) * LANES] = slab_ref[k]


def _layer_kernel(x_ref, modm_ref, modf_ref, vec_ref, win_ref, wa_ref, wx_ref, wout_ref,
                  wgu_ref, wd_ref, o_ref,
                  slab_in_ref, slab_out_ref, xp_ref, x1_ref, ya_ref, u_ref, a_ref, bx_ref, mrg_ref,
                  hcar_ref, vcar_ref, rcar_ref, *, tiles_per_seq, final_norm):
    s = pl.program_id(0)
    nt = s % tiles_per_seq
    slot = s % 2

    @pl.when(s == 0)
    def _():
        x1_ref[1] = jnp.zeros(x1_ref.shape[1:], x1_ref.dtype)

    @pl.when(nt == 0)
    def _():
        hcar_ref[...] = jnp.zeros_like(hcar_ref)
        vcar_ref[...] = jnp.zeros_like(vcar_ref)
        rcar_ref[...] = jnp.zeros_like(rcar_ref)

    xp = _load_permuted(x_ref, slab_in_ref)
    xp_ref[...] = xp
    modm = modm_ref[0]
    modf = modf_ref[0]
    h_mix = _rms_modulate(xp, vec_ref[ROW_GMIX:ROW_GMIX + 1, :],
                          modm[0:1], modm[1:2]).astype(jnp.bfloat16)
    h_ffn = _rms_modulate(x1_ref[1 - slot], vec_ref[ROW_GFFN:ROW_GFFN + 1, :],
                          modf[3:4], modf[4:5]).astype(jnp.bfloat16)

    ffn_out = None
    dff = wd_ref.shape[0]
    for hd, (lo, hi) in enumerate(_ffn_chunks(dff)):
        proj = _head_proj(hd, h_mix, win_ref)
        _mixer_head(hd, proj, nt == 0, vec_ref, wa_ref, wx_ref,
                    ya_ref, u_ref, a_ref, bx_ref, mrg_ref, hcar_ref, vcar_ref, rcar_ref)
        g_ff, u_ff = _ffn_gate_up(h_ffn, lo, hi, wgu_ref, dff)
        part = _ffn_down(g_ff, u_ff, lo, hi, wd_ref)
        ffn_out = part if ffn_out is None else ffn_out + part
    mix_out = jnp.dot(mrg_ref[...], wout_ref[...], preferred_element_type=jnp.float32)

    x2 = x1_ref[1 - slot] + modf[5:6] * ffn_out
    if final_norm:
        ms2 = jnp.mean(x2 * x2, axis=-1, keepdims=True)
        x2 = x2 * lax.rsqrt(ms2 + EPS) * vec_ref[ROW_GFINAL:ROW_GFINAL + 1, :]
    _store_time_order(x2, o_ref, slab_out_ref)

    x1_ref[slot] = xp_ref[...] + modm[2:3] * mix_out


def _layer(x, mod, vecs, w_in_heads, w_rg_a, w_rg_x, w_out, w_gate_up, w_down, final_norm):
    b, s, d = x.shape
    nts = s // TM
    n_tiles = b * nts
    const = dict(pipeline_mode=pl.Buffered(1))

    def tile(i, shift):
        t = jnp.clip(i + shift, 0, n_tiles - 1)
        return t // nts, t % nts

    def x_spec(shift):
        return pl.BlockSpec((1, TM, d), lambda i: (*tile(i, shift), 0))

    def mod_spec(shift):
        return pl.BlockSpec((1, N_MOD, d), lambda i: (tile(i, shift)[0], 0, 0))

    def whole(arr):
        return pl.BlockSpec(arr.shape, lambda i: (0,) * arr.ndim, **const)

    f32, bf16 = jnp.float32, jnp.bfloat16
    return pl.pallas_call(
        functools.partial(_layer_kernel, tiles_per_seq=nts, final_norm=final_norm),
        grid=(n_tiles + 1,),
        in_specs=[x_spec(0), mod_spec(0), mod_spec(-1),
                  whole(vecs), whole(w_in_heads), whole(w_rg_a), whole(w_rg_x), whole(w_out),
                  whole(w_gate_up), whole(w_down)],
        out_specs=x_spec(-1),
        out_shape=jax.ShapeDtypeStruct((b, s, d), f32),
        scratch_shapes=[pltpu.VMEM((d // LANES, TM, LANES), f32),
                        pltpu.VMEM((d // LANES, TM, LANES), f32),
                        pltpu.VMEM((TM, d), f32),
                        pltpu.VMEM((2, TM, d), f32),
                        pltpu.VMEM((TM, d), f32),
                        pltpu.VMEM((TM, d), f32),
                        pltpu.VMEM((TM, d), f32),
                        pltpu.VMEM((TM, d), f32),
                        pltpu.VMEM((TM, d), bf16),
                        pltpu.VMEM((8, d), f32),
                        pltpu.VMEM((8, d), f32),
                        pltpu.VMEM((8, d), f32)],
        compiler_params=pltpu.CompilerParams(
            dimension_semantics=("arbitrary",),
            vmem_limit_bytes=VMEM_LIMIT_BYTES),
        name="layer",
    )(x, mod, mod, vecs, w_in_heads, w_rg_a, w_rg_x, w_out, w_gate_up, w_down)


def kernel(x, c, w_ada, b_ada, g_norm_mix, w_in, conv_a_w, conv_b_w, conv_b_bias,
           w_rg_a, b_rg_a, w_rg_x, b_rg_x, lru_lambda, w_out, g_norm_ffn,
           w_gate_up, w_down, g_norm_final):
    depth = w_ada.shape[0]
    b, s, d = x.shape
    hw = d // RNN_HEADS
    assert s % TM == 0 and TM % (ROWS * TIME_BLOCKS) == 0 and d % LANES == 0
    assert w_in.shape[2] == N_PROJ * d and w_rg_a.shape[1:] == (RNN_HEADS, hw, hw)
    assert w_down.shape[1] % MXU_TILE == 0
    bf16 = jnp.bfloat16
    for l in range(depth):
        mod = _adaln(c, w_ada[l], b_ada[l])
        vecs = jnp.concatenate(
            [g_norm_mix[l][None], conv_a_w[l], conv_b_w[l], conv_b_bias[l][None],
             b_rg_a[l][None], b_rg_x[l][None], lru_lambda[l][None], g_norm_ffn[l][None],
             g_norm_final[None], jnp.zeros((N_VEC_ROWS - ROW_GFINAL - 1, d), jnp.float32)],
            axis=0)
        x = _layer(x, mod, vecs, w_in[l].astype(bf16), w_rg_a[l].astype(bf16), w_rg_x[l].astype(bf16),
                   w_out[l].astype(bf16), w_gate_up[l].astype(bf16), w_down[l].astype(bf16),
                   final_norm=(l == depth - 1))
    return x
```

```python
import functools

import jax
import jax.numpy as jnp
from jax import lax
from jax.experimental import pallas as pl
from jax.experimental.pallas import tpu as pltpu

EPS = 1e-6
LRU_C = 8.0
N_MOD = 6
RNN_HEADS = 4
N_PROJ = 7
P_CB, P_CC, P_CX, P_RX, P_RG, P_GA, P_GB = range(N_PROJ)
TIME_BLOCKS = 8
LANES = 128
MXU_TILE = 256
ROWS = 16
TM = 256
PREPARE_AFTER_HEAD = 1
VMEM_LIMIT_BYTES = 60 * 1024 * 1024

(ROW_GMIX, ROW_CA, ROW_CB, ROW_CBIAS, ROW_BA, ROW_BX, ROW_LAM, ROW_GFFN, ROW_GFINAL,
 N_VEC_ROWS) = 0, 1, 4, 8, 9, 10, 11, 12, 13, 16


def _sigmoid(x):
    return 0.5 * jnp.tanh(0.5 * x) + 0.5


def _shift_down_one_row(blk, carry_row, first_row):
    return jnp.where(first_row, carry_row, pltpu.roll(blk, 1, 0))


def _row_scan(a, b):
    n = a.shape[0]
    rows = lax.broadcasted_iota(jnp.int32, a.shape, 0)
    k = 1
    while k < n:
        keep = rows >= k
        a_sh = jnp.where(keep, pltpu.roll(a, k, 0), 1.0)
        b_sh = jnp.where(keep, pltpu.roll(b, k, 0), 0.0)
        b = a * b_sh + b
        a = a * a_sh
        k *= 2
    return a, b


def _adaln_kernel(c_ref, w_ref, b_ref, o_ref):
    c_act = jax.nn.silu(c_ref[...]).astype(jnp.bfloat16)
    o_ref[...] = jnp.dot(c_act, w_ref[...].astype(jnp.bfloat16),
                         preferred_element_type=jnp.float32) + b_ref[...]


def _adaln(c, w_ada, b_ada):
    b, d = c.shape
    n = w_ada.shape[1]
    rows = 8
    c_pad = jnp.zeros((rows, d), jnp.float32).at[:b].set(c)
    bn = d
    out = pl.pallas_call(
        _adaln_kernel,
        grid=(n // bn,),
        in_specs=[pl.BlockSpec((rows, d), lambda i: (0, 0)),
                  pl.BlockSpec((d, bn), lambda i: (0, i)),
                  pl.BlockSpec((1, bn), lambda i: (0, i))],
        out_specs=pl.BlockSpec((rows, bn), lambda i: (0, i)),
        out_shape=jax.ShapeDtypeStruct((rows, n), jnp.float32),
        name="adaln",
    )(c_pad, w_ada, b_ada.reshape(1, n))
    return out[:b].reshape(b, N_MOD, d)


def _rms_modulate(x, gain, shift, scale):
    ms = jnp.mean(x * x, axis=-1, keepdims=True)
    return (x * lax.rsqrt(ms + EPS) * gain) * (1.0 + scale) + shift


def _ffn_chunks(dff):
    tiles = dff // MXU_TILE
    bounds = [MXU_TILE * ((tiles * i + RNN_HEADS - 1) // RNN_HEADS) for i in range(RNN_HEADS + 1)]
    return list(zip(bounds[:-1], bounds[1:]))


def _head_proj(hd, h, win_ref):
    d = h.shape[1]
    hw = d // RNN_HEADS
    return jnp.concatenate(
        [jnp.dot(h, win_ref[:, i * d + hd * hw:i * d + (hd + 1) * hw],
                 preferred_element_type=jnp.float32) for i in range(N_PROJ)], axis=1)


def _mixer_head(hd, proj, seq_start_tile, vec_ref, wa_ref, wx_ref,
                ya_ref, u_ref, a_ref, bx_ref, mrg_ref, hcar_ref, vcar_ref, rcar_ref):
    tm = proj.shape[0]
    g_rows = tm // TIME_BLOCKS
    hw = proj.shape[1] // N_PROJ
    n_gc = g_rows // ROWS
    cs = slice(hd * hw, (hd + 1) * hw)
    first_row = lax.broadcasted_iota(jnp.int32, (g_rows, hw), 0) == 0
    seq_start = jnp.logical_and(lax.broadcasted_iota(jnp.int32, (ROWS, hw), 0) == 0,
                                seq_start_tile)

    def block(i, j):
        return proj[j * g_rows:(j + 1) * g_rows, i * hw:(i + 1) * hw]

    def rows(j, gc):
        return slice(j * g_rows + gc * ROWS, j * g_rows + (gc + 1) * ROWS)

    def item(i, j, gc):
        return proj[rows(j, gc), i * hw:(i + 1) * hw]

    def wrapped(blks, carry_ref):
        out = [_shift_down_one_row(b, carry_ref[r:r + 1, cs], first_row)
               for r, b in enumerate(blks)]
        for r, b in enumerate(blks):
            carry_ref[r:r + 1, cs] = b[g_rows - 1:g_rows]
        return out

    def vec(row):
        return vec_ref[row:row + 1, cs]

    wca = [vec(ROW_CA + k) for k in range(3)]
    wcb = [vec(ROW_CB + k) for k in range(4)]
    bias = vec(ROW_CBIAS)
    v_wrap = wrapped([block(P_CC, j) * block(P_CX, j) for j in (6, 7)], vcar_ref)
    r_wrap = wrapped([block(P_RX, j) for j in (5, 6, 7)], rcar_ref)
    for gc in range(n_gc):
        gsl = slice(gc * ROWS, (gc + 1) * ROWS)
        v_hist = [w[gsl] for w in v_wrap]
        r_hist = [w[gsl] for w in r_wrap]
        for j in range(TIME_BLOCKS):
            v = item(P_CC, j, gc) * item(P_CX, j, gc)
            conv_a = wca[0] * v_hist[0] + wca[1] * v_hist[1] + wca[2] * v
            ya_ref[rows(j, gc), cs] = item(P_CB, j, gc) * conv_a
            v_hist = [v_hist[1], v]
            rx = item(P_RX, j, gc)
            u_ref[rows(j, gc), cs] = (wcb[0] * r_hist[0] + wcb[1] * r_hist[1]
                                      + wcb[2] * r_hist[2] + wcb[3] * rx + bias)
            r_hist = [r_hist[1], r_hist[2], rx]

    u16 = u_ref[:, cs].astype(jnp.bfloat16)
    r_pre = jnp.dot(u16, wa_ref[hd], preferred_element_type=jnp.float32)
    i_pre = jnp.dot(u16, wx_ref[hd], preferred_element_type=jnp.float32)
    b_a, b_x = vec(ROW_BA), vec(ROW_BX)
    log_a_scale = LRU_C * jax.nn.log_sigmoid(vec(ROW_LAM))

    a_runs, b_runs = [], []
    for gc in range(n_gc):
        for j in range(TIME_BLOCKS):
            rw = rows(j, gc)
            r = _sigmoid(r_pre[rw] + b_a)
            i = _sigmoid(i_pre[rw] + b_x)
            a = jnp.exp(r * log_a_scale)
            mult = jnp.sqrt(jnp.maximum(1.0 - a * a, 0.0))
            if j == 0 and gc == 0:
                mult = jnp.where(seq_start, 1.0, mult)
            bx = mult * (i * u_ref[rw, cs])
            a_ref[rw, cs] = a
            bx_ref[rw, cs] = bx
            if j == 0:
                a_run, b_run = a, bx
            else:
                b_run = a * b_run + bx
                a_run = a * a_run
        a_runs.append(a_run)
        b_runs.append(b_run)
    a_grp, b_grp = _row_scan(jnp.concatenate(a_runs, axis=0), jnp.concatenate(b_runs, axis=0))
    h0 = hcar_ref[0:1, cs]
    h_end = b_grp + a_grp * h0
    h_in = _shift_down_one_row(h_end, h0, first_row)
    hcar_ref[0:1, cs] = h_end[g_rows - 1:g_rows]

    for gc in range(n_gc):
        h = h_in[gc * ROWS:(gc + 1) * ROWS]
        for j in range(TIME_BLOCKS):
            rw = rows(j, gc)
            h = a_ref[rw, cs] * h + bx_ref[rw, cs]
            y_b = h * jax.nn.gelu(item(P_RG, j, gc))
            merged = (_sigmoid(item(P_GA, j, gc)) * ya_ref[rw, cs]
                      + _sigmoid(item(P_GB, j, gc)) * y_b)
            mrg_ref[rw, cs] = merged.astype(jnp.bfloat16)


def _ffn_gate_up(h, lo, hi, wgu_ref, dff):
    g_ff = jnp.dot(h, wgu_ref[:, lo:hi], preferred_element_type=jnp.float32)
    u_ff = jnp.dot(h, wgu_ref[:, dff + lo:dff + hi], preferred_element_type=jnp.float32)
    return g_ff, u_ff


def _ffn_down(g_ff, u_ff, lo, hi, wd_ref):
    act = (g_ff * _sigmoid(g_ff) * u_ff).astype(jnp.bfloat16)
    return jnp.dot(act, wd_ref[lo:hi, :], preferred_element_type=jnp.float32)


def _load_permuted(x_ref, slab_ref):
    tm, d = x_ref.shape[1], x_ref.shape[2]
    g_rows = tm // TIME_BLOCKS
    for k in range(d // LANES):
        slab_ref[k] = x_ref[0, :, k * LANES:(k + 1) * LANES]
    return jnp.concatenate(
        [jnp.concatenate([slab_ref[k, pl.ds(j, g_rows, stride=TIME_BLOCKS), :]
                          for k in range(d // LANES)], axis=1)
         for j in range(TIME_BLOCKS)], axis=0)


def _store_time_order(x2, o_ref, slab_ref):
    tm, d = x2.shape
    g_rows = tm // TIME_BLOCKS
    for k in range(d // LANES):
        for j in range(TIME_BLOCKS):
            slab_ref[k, pl.ds(j, g_rows, stride=TIME_BLOCKS), :] = (
                x2[j * g_rows:(j + 1) * g_rows, k * LANES:(k + 1) * LANES])
        o_ref[0, :, k * LANES:(k + 1) * LANES] = slab_ref[k]


def _layer_kernel(xn_ref, x0_ref, modn_ref, modm_ref, modf_ref, vec_ref, win_ref, wa_ref, wx_ref,
                  wout_ref, wgu_ref, wd_ref, o_ref,
                  slab_in_ref, slab_out_ref, xp_ref, hmix_ref, x1_ref,
                  ya_ref, u_ref, a_ref, bx_ref, mrg_ref, hcar_ref, vcar_ref, rcar_ref,
                  *, tiles_per_seq, final_norm):
    s = pl.program_id(0)
    nt = s % tiles_per_seq
    slot = s % 2

    def prepare(x_ref, mod, dst):
        xp = _load_permuted(x_ref, slab_in_ref)
        xp_ref[dst] = xp
        hmix_ref[dst] = _rms_modulate(xp, vec_ref[ROW_GMIX:ROW_GMIX + 1, :],
                                      mod[0:1], mod[1:2]).astype(jnp.bfloat16)

    @pl.when(s == 0)
    def _():
        prepare(x0_ref, modm_ref[0], 0)
        x1_ref[1] = jnp.zeros(x1_ref.shape[1:], x1_ref.dtype)

    @pl.when(nt == 0)
    def _():
        hcar_ref[...] = jnp.zeros_like(hcar_ref)
        vcar_ref[...] = jnp.zeros_like(vcar_ref)
        rcar_ref[...] = jnp.zeros_like(rcar_ref)

    modm = modm_ref[0]
    modf = modf_ref[0]
    h_ffn = _rms_modulate(x1_ref[1 - slot], vec_ref[ROW_GFFN:ROW_GFFN + 1, :],
                          modf[3:4], modf[4:5]).astype(jnp.bfloat16)

    ffn_out = None
    dff = wd_ref.shape[0]
    for hd, (lo, hi) in enumerate(_ffn_chunks(dff)):
        proj = _head_proj(hd, hmix_ref[slot], win_ref)
        _mixer_head(hd, proj, nt == 0, vec_ref, wa_ref, wx_ref,
                    ya_ref, u_ref, a_ref, bx_ref, mrg_ref, hcar_ref, vcar_ref, rcar_ref)
        g_ff, u_ff = _ffn_gate_up(h_ffn, lo, hi, wgu_ref, dff)
        part = _ffn_down(g_ff, u_ff, lo, hi, wd_ref)
        ffn_out = part if ffn_out is None else ffn_out + part
        if hd == PREPARE_AFTER_HEAD:
            prepare(xn_ref, modn_ref[0], 1 - slot)
    mix_out = jnp.dot(mrg_ref[...], wout_ref[...], preferred_element_type=jnp.float32)

    x2 = x1_ref[1 - slot] + modf[5:6] * ffn_out
    if final_norm:
        ms2 = jnp.mean(x2 * x2, axis=-1, keepdims=True)
        x2 = x2 * lax.rsqrt(ms2 + EPS) * vec_ref[ROW_GFINAL:ROW_GFINAL + 1, :]
    _store_time_order(x2, o_ref, slab_out_ref)

    x1_ref[slot] = xp_ref[slot] + modm[2:3] * mix_out


def _layer(x, mod, vecs, w_in_heads, w_rg_a, w_rg_x, w_out, w_gate_up, w_down, final_norm):
    b, s, d = x.shape
    nts = s // TM
    n_tiles = b * nts
    const = dict(pipeline_mode=pl.Buffered(1))

    def tile(i, shift):
        t = jnp.clip(i + shift, 0, n_tiles - 1)
        return t // nts, t % nts

    def x_spec(shift):
        return pl.BlockSpec((1, TM, d), lambda i: (*tile(i, shift), 0))

    def mod_spec(shift):
        return pl.BlockSpec((1, N_MOD, d), lambda i: (tile(i, shift)[0], 0, 0))

    def whole(arr):
        return pl.BlockSpec(arr.shape, lambda i: (0,) * arr.ndim, **const)

    f32, bf16 = jnp.float32, jnp.bfloat16
    return pl.pallas_call(
        functools.partial(_layer_kernel, tiles_per_seq=nts, final_norm=final_norm),
        grid=(n_tiles + 1,),
        in_specs=[x_spec(1), pl.BlockSpec((1, TM, d), lambda i: (0, 0, 0), **const),
                  mod_spec(1), mod_spec(0), mod_spec(-1),
                  whole(vecs), whole(w_in_heads), whole(w_rg_a), whole(w_rg_x), whole(w_out),
                  whole(w_gate_up), whole(w_down)],
        out_specs=x_spec(-1),
        out_shape=jax.ShapeDtypeStruct((b, s, d), f32),
        scratch_shapes=[pltpu.VMEM((d // LANES, TM, LANES), f32),
                        pltpu.VMEM((d // LANES, TM, LANES), f32),
                        pltpu.VMEM((2, TM, d), f32),
                        pltpu.VMEM((2, TM, d), bf16),
                        pltpu.VMEM((2, TM, d), f32),
                        pltpu.VMEM((TM, d), f32),
                        pltpu.VMEM((TM, d), f32),
                        pltpu.VMEM((TM, d), f32),
                        pltpu.VMEM((TM, d), f32),
                        pltpu.VMEM((TM, d), bf16),
                        pltpu.VMEM((8, d), f32),
                        pltpu.VMEM((8, d), f32),
                        pltpu.VMEM((8, d), f32)],
        compiler_params=pltpu.CompilerParams(
            dimension_semantics=("arbitrary",),
            vmem_limit_bytes=VMEM_LIMIT_BYTES),
        name="layer",
    )(x, x, mod, mod, mod, vecs, w_in_heads, w_rg_a, w_rg_x, w_out, w_gate_up, w_down)


def kernel(x, c, w_ada, b_ada, g_norm_mix, w_in, conv_a_w, conv_b_w, conv_b_bias,
           w_rg_a, b_rg_a, w_rg_x, b_rg_x, lru_lambda, w_out, g_norm_ffn,
           w_gate_up, w_down, g_norm_final):
    depth = w_ada.shape[0]
    b, s, d = x.shape
    hw = d // RNN_HEADS
    assert s % TM == 0 and TM % (ROWS * TIME_BLOCKS) == 0 and d % LANES == 0
    assert w_in.shape[2] == N_PROJ * d and w_rg_a.shape[1:] == (RNN_HEADS, hw, hw)
    assert w_down.shape[1] % MXU_TILE == 0
    bf16 = jnp.bfloat16
    for l in range(depth):
        mod = _adaln(c, w_ada[l], b_ada[l])
        vecs = jnp.concatenate(
            [g_norm_mix[l][None], conv_a_w[l], conv_b_w[l], conv_b_bias[l][None],
             b_rg_a[l][None], b_rg_x[l][None], lru_lambda[l][None], g_norm_ffn[l][None],
             g_norm_final[None], jnp.zeros((N_VEC_ROWS - ROW_GFINAL - 1, d), jnp.float32)],
            axis=0)
        x = _layer(x, mod, vecs, w_in[l].astype(bf16), w_rg_a[l].astype(bf16), w_rg_x[l].astype(bf16),
                   w_out[l].astype(bf16), w_gate_up[l].astype(bf16), w_down[l].astype(bf16),
                   final_norm=(l == depth - 1))
    return x
```

```python
import functools

import jax
import jax.numpy as jnp
from jax import lax
from jax.experimental import pallas as pl
from jax.experimental.pallas import tpu as pltpu

EPS = 1e-6
LRU_C = 8.0
N_MOD = 6
RNN_HEADS = 4
N_PROJ = 7
P_CB, P_CC, P_CX, P_RX, P_RG, P_GA, P_GB = range(N_PROJ)
TIME_BLOCKS = 8
LANES = 128
MXU_TILE = 256
ROWS = 16
TM = 256
VMEM_LIMIT_BYTES = 60 * 1024 * 1024

(ROW_GMIX, ROW_CA, ROW_CB, ROW_CBIAS, ROW_BA, ROW_BX, ROW_LAM, ROW_GFFN, ROW_GFINAL,
 N_VEC_ROWS) = 0, 1, 4, 8, 9, 10, 11, 12, 13, 16


def _sigmoid(x):
    return 0.5 * jnp.tanh(0.5 * x) + 0.5


def _shift_down_one_row(blk, carry_row, first_row):
    return jnp.where(first_row, carry_row, pltpu.roll(blk, 1, 0))


def _row_scan(a, b):
    n = a.shape[0]
    rows = lax.broadcasted_iota(jnp.int32, a.shape, 0)
    k = 1
    while k < n:
        keep = rows >= k
        a_sh = jnp.where(keep, pltpu.roll(a, k, 0), 1.0)
        b_sh = jnp.where(keep, pltpu.roll(b, k, 0), 0.0)
        b = a * b_sh + b
        a = a * a_sh
        k *= 2
    return a, b


def _adaln_kernel(c_ref, w_ref, b_ref, o_ref):
    c_act = jax.nn.silu(c_ref[...]).astype(jnp.bfloat16)
    o_ref[...] = jnp.dot(c_act, w_ref[...].astype(jnp.bfloat16),
                         preferred_element_type=jnp.float32) + b_ref[...]


def _adaln(c, w_ada, b_ada):
    b, d = c.shape
    n = w_ada.shape[1]
    rows = 8
    c_pad = jnp.zeros((rows, d), jnp.float32).at[:b].set(c)
    bn = d
    out = pl.pallas_call(
        _adaln_kernel,
        grid=(n // bn,),
        in_specs=[pl.BlockSpec((rows, d), lambda i: (0, 0)),
                  pl.BlockSpec((d, bn), lambda i: (0, i)),
                  pl.BlockSpec((1, bn), lambda i: (0, i))],
        out_specs=pl.BlockSpec((rows, bn), lambda i: (0, i)),
        out_shape=jax.ShapeDtypeStruct((rows, n), jnp.float32),
        name="adaln",
    )(c_pad, w_ada, b_ada.reshape(1, n))
    return out[:b].reshape(b, N_MOD, d)


def _rms_modulate(x, gain, shift, scale):
    ms = jnp.mean(x * x, axis=-1, keepdims=True)
    return (x * lax.rsqrt(ms + EPS) * gain) * (1.0 + scale) + shift


def _ffn_chunks(dff):
    tiles = dff // MXU_TILE
    bounds = [MXU_TILE * ((tiles * i + RNN_HEADS - 1) // RNN_HEADS) for i in range(RNN_HEADS + 1)]
    return list(zip(bounds[:-1], bounds[1:]))


def _head_proj(hd, h, win_ref):
    d = h.shape[1]
    hw = d // RNN_HEADS
    return jnp.concatenate(
        [jnp.dot(h, win_ref[:, i * d + hd * hw:i * d + (hd + 1) * hw],
                 preferred_element_type=jnp.float32) for i in range(N_PROJ)], axis=1)


def _mixer_head(hd, proj, seq_start_tile, vec_ref, wa_ref, wx_ref,
                ya_ref, u_ref, a_ref, bx_ref, mrg_ref, hcar_ref, vcar_ref, rcar_ref):
    tm = proj.shape[0]
    g_rows = tm // TIME_BLOCKS
    hw = proj.shape[1] // N_PROJ
    n_gc = g_rows // ROWS
    cs = slice(hd * hw, (hd + 1) * hw)
    first_row = lax.broadcasted_iota(jnp.int32, (g_rows, hw), 0) == 0
    seq_start = jnp.logical_and(lax.broadcasted_iota(jnp.int32, (ROWS, hw), 0) == 0,
                                seq_start_tile)

    def block(i, j):
        return proj[j * g_rows:(j + 1) * g_rows, i * hw:(i + 1) * hw]

    def rows(j, gc):
        return slice(j * g_rows + gc * ROWS, j * g_rows + (gc + 1) * ROWS)

    def item(i, j, gc):
        return proj[rows(j, gc), i * hw:(i + 1) * hw]

    def wrapped(blks, carry_ref):
        out = [_shift_down_one_row(b, carry_ref[r:r + 1, cs], first_row)
               for r, b in enumerate(blks)]
        for r, b in enumerate(blks):
            carry_ref[r:r + 1, cs] = b[g_rows - 1:g_rows]
        return out

    def vec(row):
        return vec_ref[row:row + 1, cs]

    wca = [vec(ROW_CA + k) for k in range(3)]
    wcb = [vec(ROW_CB + k) for k in range(4)]
    bias = vec(ROW_CBIAS)
    v_wrap = wrapped([block(P_CC, j) * block(P_CX, j) for j in (6, 7)], vcar_ref)
    r_wrap = wrapped([block(P_RX, j) for j in (5, 6, 7)], rcar_ref)
    for gc in range(n_gc):
        gsl = slice(gc * ROWS, (gc + 1) * ROWS)
        v_hist = [w[gsl] for w in v_wrap]
        r_hist = [w[gsl] for w in r_wrap]
        for j in range(TIME_BLOCKS):
            v = item(P_CC, j, gc) * item(P_CX, j, gc)
            conv_a = wca[0] * v_hist[0] + wca[1] * v_hist[1] + wca[2] * v
            ya_ref[rows(j, gc), cs] = item(P_CB, j, gc) * conv_a
            v_hist = [v_hist[1], v]
            rx = item(P_RX, j, gc)
            u_ref[rows(j, gc), cs] = (wcb[0] * r_hist[0] + wcb[1] * r_hist[1]
                                      + wcb[2] * r_hist[2] + wcb[3] * rx + bias)
            r_hist = [r_hist[1], r_hist[2], rx]

    u16 = u_ref[:, cs].astype(jnp.bfloat16)
    r_pre = jnp.dot(u16, wa_ref[hd], preferred_element_type=jnp.float32)
    i_pre = jnp.dot(u16, wx_ref[hd], preferred_element_type=jnp.float32)
    b_a, b_x = vec(ROW_BA), vec(ROW_BX)
    log_a_scale = LRU_C * jax.nn.log_sigmoid(vec(ROW_LAM))

    a_runs, b_runs = [], []
    for gc in range(n_gc):
        for j in range(TIME_BLOCKS):
            rw = rows(j, gc)
            r = _sigmoid(r_pre[rw] + b_a)
            i = _sigmoid(i_pre[rw] + b_x)
            a = jnp.exp(r * log_a_scale)
            mult = jnp.sqrt(jnp.maximum(1.0 - a * a, 0.0))
            if j == 0 and gc == 0:
                mult = jnp.where(seq_start, 1.0, mult)
            bx = mult * (i * u_ref[rw, cs])
            a_ref[rw, cs] = a
            bx_ref[rw, cs] = bx
            if j == 0:
                a_run, b_run = a, bx
            else:
                b_run = a * b_run + bx
                a_run = a * a_run
        a_runs.append(a_run)
        b_runs.append(b_run)
    a_grp, b_grp = _row_scan(jnp.concatenate(a_runs, axis=0), jnp.concatenate(b_runs, axis=0))
    h0 = hcar_ref[0:1, cs]
    h_end = b_grp + a_grp * h0
    h_in = _shift_down_one_row(h_end, h0, first_row)
    hcar_ref[0:1, cs] = h_end[g_rows - 1:g_rows]

    for gc in range(n_gc):
        h = h_in[gc * ROWS:(gc + 1) * ROWS]
        for j in range(TIME_BLOCKS):
            rw = rows(j, gc)
            h = a_ref[rw, cs] * h + bx_ref[rw, cs]
            y_b = h * jax.nn.gelu(item(P_RG, j, gc))
            merged = (_sigmoid(item(P_GA, j, gc)) * ya_ref[rw, cs]
                      + _sigmoid(item(P_GB, j, gc)) * y_b)
            mrg_ref[rw, cs] = merged.astype(jnp.bfloat16)


def _ffn_gate_up(h, lo, hi, wgu_ref, dff):
    g_ff = jnp.dot(h, wgu_ref[:, lo:hi], preferred_element_type=jnp.float32)
    u_ff = jnp.dot(h, wgu_ref[:, dff + lo:dff + hi], preferred_element_type=jnp.float32)
    return g_ff, u_ff


def _ffn_down(g_ff, u_ff, lo, hi, wd_ref):
    act = (g_ff * _sigmoid(g_ff) * u_ff).astype(jnp.bfloat16)
    return jnp.dot(act, wd_ref[lo:hi, :], preferred_element_type=jnp.float32)


def _tile_copies(hbm_ref, buf_ref, sem, tile, tiles_per_seq, to_hbm):
    g_rows = buf_ref.shape[0] // TIME_BLOCKS
    b = tile // tiles_per_seq
    g0 = (tile % tiles_per_seq) * g_rows
    copies = []
    for j in range(TIME_BLOCKS):
        hbm = hbm_ref.at[b, pl.ds(g0, g_rows), j, :]
        buf = buf_ref.at[pl.ds(j * g_rows, g_rows), :]
        copies.append(pltpu.make_async_copy(buf, hbm, sem) if to_hbm
                      else pltpu.make_async_copy(hbm, buf, sem))
    return copies


def _layer_kernel(x_hbm, modm_ref, modf_ref, vec_ref, win_ref, wa_ref, wx_ref, wout_ref,
                  wgu_ref, wd_ref, o_hbm,
                  xin_ref, xout_ref, x1_ref, ya_ref, u_ref, a_ref, bx_ref, mrg_ref,
                  hcar_ref, vcar_ref, rcar_ref, in_sem, out_sem,
                  *, n_tiles, tiles_per_seq, final_norm):
    s = pl.program_id(0)
    nt = s % tiles_per_seq
    slot = s % 2
    last = n_tiles

    def load(tile, buf_slot):
        return _tile_copies(x_hbm, xin_ref.at[buf_slot], in_sem.at[buf_slot], tile,
                            tiles_per_seq, to_hbm=False)

    def store(tile, buf_slot):
        return _tile_copies(o_hbm, xout_ref.at[buf_slot], out_sem.at[buf_slot], tile,
                            tiles_per_seq, to_hbm=True)

    @pl.when(s == 0)
    def _():
        for cp in load(0, 0):
            cp.start()
        x1_ref[1] = jnp.zeros(x1_ref.shape[1:], x1_ref.dtype)

    @pl.when(s + 1 < last)
    def _():
        for cp in load(s + 1, 1 - slot):
            cp.start()

    @pl.when(s < last)
    def _():
        for cp in load(s, slot):
            cp.wait()

    @pl.when(s >= 3)
    def _():
        for cp in store(s - 3, slot):
            cp.wait()

    @pl.when(nt == 0)
    def _():
        hcar_ref[...] = jnp.zeros_like(hcar_ref)
        vcar_ref[...] = jnp.zeros_like(vcar_ref)
        rcar_ref[...] = jnp.zeros_like(rcar_ref)

    modm = modm_ref[0]
    modf = modf_ref[0]
    h_mix = _rms_modulate(xin_ref[slot], vec_ref[ROW_GMIX:ROW_GMIX + 1, :],
                          modm[0:1], modm[1:2]).astype(jnp.bfloat16)
    h_ffn = _rms_modulate(x1_ref[1 - slot], vec_ref[ROW_GFFN:ROW_GFFN + 1, :],
                          modf[3:4], modf[4:5]).astype(jnp.bfloat16)

    ffn_out = None
    dff = wd_ref.shape[0]
    for hd, (lo, hi) in enumerate(_ffn_chunks(dff)):
        proj = _head_proj(hd, h_mix, win_ref)
        _mixer_head(hd, proj, nt == 0, vec_ref, wa_ref, wx_ref,
                    ya_ref, u_ref, a_ref, bx_ref, mrg_ref, hcar_ref, vcar_ref, rcar_ref)
        g_ff, u_ff = _ffn_gate_up(h_ffn, lo, hi, wgu_ref, dff)
        part = _ffn_down(g_ff, u_ff, lo, hi, wd_ref)
        ffn_out = part if ffn_out is None else ffn_out + part
    mix_out = jnp.dot(mrg_ref[...], wout_ref[...], preferred_element_type=jnp.float32)

    x2 = x1_ref[1 - slot] + modf[5:6] * ffn_out
    if final_norm:
        ms2 = jnp.mean(x2 * x2, axis=-1, keepdims=True)
        x2 = x2 * lax.rsqrt(ms2 + EPS) * vec_ref[ROW_GFINAL:ROW_GFINAL + 1, :]
    xout_ref[slot] = x2
    x1_ref[slot] = xin_ref[slot] + modm[2:3] * mix_out

    @pl.when(s >= 1)
    def _():
        for cp in store(s - 1, slot):
            cp.start()

    @pl.when(s == last)
    def _():
        for cp in store(s - 2, 1 - slot):
            cp.wait()
        for cp in store(s - 1, slot):
            cp.wait()


def _layer(x, mod, vecs, w_in_heads, w_rg_a, w_rg_x, w_out, w_gate_up, w_down, final_norm):
    b, s, d = x.shape
    nts = s // TM
    n_tiles = b * nts
    assert n_tiles >= 2
    const = dict(pipeline_mode=pl.Buffered(1))

    def mod_spec(shift):
        return pl.BlockSpec(
            (1, N_MOD, d), lambda i: (jnp.clip(i + shift, 0, n_tiles - 1) // nts, 0, 0))

    def whole(arr):
        return pl.BlockSpec(arr.shape, lambda i: (0,) * arr.ndim, **const)

    f32, bf16 = jnp.float32, jnp.bfloat16
    out = pl.pallas_call(
        functools.partial(_layer_kernel, n_tiles=n_tiles, tiles_per_seq=nts,
                          final_norm=final_norm),
        grid=(n_tiles + 1,),
        in_specs=[pl.BlockSpec(memory_space=pl.ANY), mod_spec(0), mod_spec(-1),
                  whole(vecs), whole(w_in_heads), whole(w_rg_a), whole(w_rg_x), whole(w_out),
                  whole(w_gate_up), whole(w_down)],
        out_specs=pl.BlockSpec(memory_space=pl.ANY),
        out_shape=jax.ShapeDtypeStruct((b, s // TIME_BLOCKS, TIME_BLOCKS, d), f32),
        scratch_shapes=[pltpu.VMEM((2, TM, d), f32),
                        pltpu.VMEM((2, TM, d), f32),
                        pltpu.VMEM((2, TM, d), f32),
                        pltpu.VMEM((TM, d), f32),
                        pltpu.VMEM((TM, d), f32),
                        pltpu.VMEM((TM, d), f32),
                        pltpu.VMEM((TM, d), f32),
                        pltpu.VMEM((TM, d), bf16),
                        pltpu.VMEM((8, d), f32),
                        pltpu.VMEM((8, d), f32),
                        pltpu.VMEM((8, d), f32),
                        pltpu.SemaphoreType.DMA((2,)),
                        pltpu.SemaphoreType.DMA((2,))],
        compiler_params=pltpu.CompilerParams(
            dimension_semantics=("arbitrary",),
            vmem_limit_bytes=VMEM_LIMIT_BYTES),
        name="layer",
    )(x.reshape(b, s // TIME_BLOCKS, TIME_BLOCKS, d), mod, mod, vecs, w_in_heads,
      w_rg_a, w_rg_x, w_out, w_gate_up, w_down)
    return out.reshape(b, s, d)


def kernel(x, c, w_ada, b_ada, g_norm_mix, w_in, conv_a_w, conv_b_w, conv_b_bias,
           w_rg_a, b_rg_a, w_rg_x, b_rg_x, lru_lambda, w_out, g_norm_ffn,
           w_gate_up, w_down, g_norm_final):
    depth = w_ada.shape[0]
    b, s, d = x.shape
    hw = d // RNN_HEADS
    assert s % TM == 0 and TM % (ROWS * TIME_BLOCKS) == 0 and d % LANES == 0
    assert w_in.shape[2] == N_PROJ * d and w_rg_a.shape[1:] == (RNN_HEADS, hw, hw)
    assert w_down.shape[1] % MXU_TILE == 0
    bf16 = jnp.bfloat16
    for l in range(depth):
        mod = _adaln(c, w_ada[l], b_ada[l])
        vecs = jnp.concatenate(
            [g_norm_mix[l][None], conv_a_w[l], conv_b_w[l], conv_b_bias[l][None],
             b_rg_a[l][None], b_rg_x[l][None], lru_lambda[l][None], g_norm_ffn[l][None],
             g_norm_final[None], jnp.zeros((N_VEC_ROWS - ROW_GFINAL - 1, d), jnp.float32)],
            axis=0)
        x = _layer(x, mod, vecs, w_in[l].astype(bf16), w_rg_a[l].astype(bf16), w_rg_x[l].astype(bf16),
                   w_out[l].astype(bf16), w_gate_up[l].astype(bf16), w_down[l].astype(bf16),
                   final_norm=(l == depth - 1))
    return x
```

```python
import functools

import jax
import jax.numpy as jnp
from jax import lax
from jax.experimental import pallas as pl
from jax.experimental.pallas import tpu as pltpu

EPS = 1e-6
LRU_C = 8.0
N_MOD = 6
RNN_HEADS = 4
N_PROJ = 7
P_CB, P_CC, P_CX, P_RX, P_RG, P_GA, P_GB = range(N_PROJ)
TIME_BLOCKS = 8
LANES = 128
MXU_TILE = 256
ROWS = 16
TM = 256
W_ROWS, W_COLS = 256, 2048
W_SLOTS = 4
N_XIN = 3
PREPARE_AFTER_HEAD = 1
VMEM_LIMIT_BYTES = 60 * 1024 * 1024

(ROW_GMIX, ROW_CA, ROW_CB, ROW_CBIAS, ROW_BA, ROW_BX, ROW_LAM, ROW_GFFN, ROW_GFINAL,
 N_VEC_ROWS) = 0, 1, 4, 8, 9, 10, 11, 12, 13, 16


def _sigmoid(x):
    return 0.5 * jnp.tanh(0.5 * x) + 0.5


def _shift_down_one_row(blk, carry_row, first_row):
    return jnp.where(first_row, carry_row, pltpu.roll(blk, 1, 0))


def _row_scan(a, b):
    n = a.shape[0]
    rows = lax.broadcasted_iota(jnp.int32, a.shape, 0)
    k = 1
    while k < n:
        keep = rows >= k
        a_sh = jnp.where(keep, pltpu.roll(a, k, 0), 1.0)
        b_sh = jnp.where(keep, pltpu.roll(b, k, 0), 0.0)
        b = a * b_sh + b
        a = a * a_sh
        k *= 2
    return a, b


def _adaln_kernel(c_ref, w_ref, b_ref, o_ref):
    c_act = jax.nn.silu(c_ref[...]).astype(jnp.bfloat16)
    o_ref[...] = jnp.dot(c_act, w_ref[...].astype(jnp.bfloat16),
                         preferred_element_type=jnp.float32) + b_ref[...]


def _adaln(c, w_ada, b_ada):
    b, d = c.shape
    n = w_ada.shape[1]
    rows = 8
    c_pad = jnp.zeros((rows, d), jnp.float32).at[:b].set(c)
    bn = d
    out = pl.pallas_call(
        _adaln_kernel,
        grid=(n // bn,),
        in_specs=[pl.BlockSpec((rows, d), lambda i: (0, 0)),
                  pl.BlockSpec((d, bn), lambda i: (0, i)),
                  pl.BlockSpec((1, bn), lambda i: (0, i))],
        out_specs=pl.BlockSpec((rows, bn), lambda i: (0, i)),
        out_shape=jax.ShapeDtypeStruct((rows, n), jnp.float32),
        name="adaln",
    )(c_pad, w_ada, b_ada.reshape(1, n))
    return out[:b].reshape(b, N_MOD, d)


def _rms_modulate(x, gain, shift, scale):
    ms = jnp.mean(x * x, axis=-1, keepdims=True)
    return (x * lax.rsqrt(ms + EPS) * gain) * (1.0 + scale) + shift


def _ffn_chunks(dff):
    tiles = dff // MXU_TILE
    bounds = [MXU_TILE * ((tiles * i + RNN_HEADS - 1) // RNN_HEADS) for i in range(RNN_HEADS + 1)]
    return list(zip(bounds[:-1], bounds[1:]))


def _head_proj(hd, h, win_ref):
    d = h.shape[1]
    hw = d // RNN_HEADS
    return jnp.concatenate(
        [jnp.dot(h, win_ref[:, i * d + hd * hw:i * d + (hd + 1) * hw],
                 preferred_element_type=jnp.float32) for i in range(N_PROJ)], axis=1)


def _mixer_head(hd, proj, seq_start_tile, vec_ref, wa_ref, wx_ref,
                ya_ref, u_ref, a_ref, bx_ref, mrg_ref, hcar_ref, vcar_ref, rcar_ref):
    tm = proj.shape[0]
    g_rows = tm // TIME_BLOCKS
    hw = proj.shape[1] // N_PROJ
    n_gc = g_rows // ROWS
    cs = slice(hd * hw, (hd + 1) * hw)
    first_row = lax.broadcasted_iota(jnp.int32, (g_rows, hw), 0) == 0
    seq_start = jnp.logical_and(lax.broadcasted_iota(jnp.int32, (ROWS, hw), 0) == 0,
                                seq_start_tile)

    def block(i, j):
        return proj[j * g_rows:(j + 1) * g_rows, i * hw:(i + 1) * hw]

    def rows(j, gc):
        return slice(j * g_rows + gc * ROWS, j * g_rows + (gc + 1) * ROWS)

    def item(i, j, gc):
        return proj[rows(j, gc), i * hw:(i + 1) * hw]

    def wrapped(blks, carry_ref):
        out = [_shift_down_one_row(b, carry_ref[r:r + 1, cs], first_row)
               for r, b in enumerate(blks)]
        for r, b in enumerate(blks):
            carry_ref[r:r + 1, cs] = b[g_rows - 1:g_rows]
        return out

    def vec(row):
        return vec_ref[row:row + 1, cs]

    wca = [vec(ROW_CA + k) for k in range(3)]
    wcb = [vec(ROW_CB + k) for k in range(4)]
    bias = vec(ROW_CBIAS)
    v_wrap = wrapped([block(P_CC, j) * block(P_CX, j) for j in (6, 7)], vcar_ref)
    r_wrap = wrapped([block(P_RX, j) for j in (5, 6, 7)], rcar_ref)
    for gc in range(n_gc):
        gsl = slice(gc * ROWS, (gc + 1) * ROWS)
        v_hist = [w[gsl] for w in v_wrap]
        r_hist = [w[gsl] for w in r_wrap]
        for j in range(TIME_BLOCKS):
            v = item(P_CC, j, gc) * item(P_CX, j, gc)
            conv_a = wca[0] * v_hist[0] + wca[1] * v_hist[1] + wca[2] * v
            ya_ref[rows(j, gc), cs] = item(P_CB, j, gc) * conv_a
            v_hist = [v_hist[1], v]
            rx = item(P_RX, j, gc)
            u_ref[rows(j, gc), cs] = (wcb[0] * r_hist[0] + wcb[1] * r_hist[1]
                                      + wcb[2] * r_hist[2] + wcb[3] * rx + bias)
            r_hist = [r_hist[1], r_hist[2], rx]

    u16 = u_ref[:, cs].astype(jnp.bfloat16)
    r_pre = jnp.dot(u16, wa_ref[cs, :], preferred_element_type=jnp.float32)
    i_pre = jnp.dot(u16, wx_ref[cs, :], preferred_element_type=jnp.float32)
    b_a, b_x = vec(ROW_BA), vec(ROW_BX)
    log_a_scale = LRU_C * jax.nn.log_sigmoid(vec(ROW_LAM))

    a_runs, b_runs = [], []
    for gc in range(n_gc):
        for j in range(TIME_BLOCKS):
            rw = rows(j, gc)
            r = _sigmoid(r_pre[rw] + b_a)
            i = _sigmoid(i_pre[rw] + b_x)
            a = jnp.exp(r * log_a_scale)
            mult = jnp.sqrt(jnp.maximum(1.0 - a * a, 0.0))
            if j == 0 and gc == 0:
                mult = jnp.where(seq_start, 1.0, mult)
            bx = mult * (i * u_ref[rw, cs])
            a_ref[rw, cs] = a
            bx_ref[rw, cs] = bx
            if j == 0:
                a_run, b_run = a, bx
            else:
                b_run = a * b_run + bx
                a_run = a * a_run
        a_runs.append(a_run)
        b_runs.append(b_run)
    a_grp, b_grp = _row_scan(jnp.concatenate(a_runs, axis=0), jnp.concatenate(b_runs, axis=0))
    h0 = hcar_ref[0:1, cs]
    h_end = b_grp + a_grp * h0
    h_in = _shift_down_one_row(h_end, h0, first_row)
    hcar_ref[0:1, cs] = h_end[g_rows - 1:g_rows]

    for gc in range(n_gc):
        h = h_in[gc * ROWS:(gc + 1) * ROWS]
        for j in range(TIME_BLOCKS):
            rw = rows(j, gc)
            h = a_ref[rw, cs] * h + bx_ref[rw, cs]
            y_b = h * jax.nn.gelu(item(P_RG, j, gc))
            merged = (_sigmoid(item(P_GA, j, gc)) * ya_ref[rw, cs]
                      + _sigmoid(item(P_GB, j, gc)) * y_b)
            mrg_ref[rw, cs] = merged.astype(jnp.bfloat16)


def _ffn_gate_up(h, lo, hi, wgu_ref, dff):
    g_ff = jnp.dot(h, wgu_ref[:, lo:hi], preferred_element_type=jnp.float32)
    u_ff = jnp.dot(h, wgu_ref[:, dff + lo:dff + hi], preferred_element_type=jnp.float32)
    return g_ff, u_ff


def _ffn_down(g_ff, u_ff, lo, hi, wd_ref):
    act = (g_ff * _sigmoid(g_ff) * u_ff).astype(jnp.bfloat16)
    return jnp.dot(act, wd_ref[lo:hi, :], preferred_element_type=jnp.float32)


def _tile_copies(hbm_ref, buf_ref, sem, tile, tiles_per_seq, to_hbm):
    g_rows = buf_ref.shape[0] // TIME_BLOCKS
    b = tile // tiles_per_seq
    g0 = (tile % tiles_per_seq) * g_rows
    copies = []
    for j in range(TIME_BLOCKS):
        hbm = hbm_ref.at[b, pl.ds(g0, g_rows), j, :]
        buf = buf_ref.at[pl.ds(j * g_rows, g_rows), :]
        copies.append(pltpu.make_async_copy(buf, hbm, sem) if to_hbm
                      else pltpu.make_async_copy(hbm, buf, sem))
    return copies


def _load_weights_bf16(pairs, stage_ref, sem):
    n_slots = stage_ref.shape[0]
    chunks = []
    for src, dst in pairs:
        n_rows, n_cols = dst.shape
        for r0 in range(0, n_rows, W_ROWS):
            for c0 in range(0, n_cols, W_COLS):
                chunks.append((src, dst, r0, c0, min(W_COLS, n_cols - c0)))

    def copy(k):
        src, _, r0, c0, cw = chunks[k]
        return pltpu.make_async_copy(src.at[pl.ds(r0, W_ROWS), pl.ds(c0, cw)],
                                     stage_ref.at[k % n_slots, :, pl.ds(0, cw)],
                                     sem.at[k % n_slots])

    ahead = n_slots - 1
    for k in range(min(ahead, len(chunks))):
        copy(k).start()
    for k, (_, dst, r0, c0, cw) in enumerate(chunks):
        if k + ahead < len(chunks):
            copy(k + ahead).start()
        copy(k).wait()
        dst[r0:r0 + W_ROWS, c0:c0 + cw] = stage_ref[k % n_slots, :, 0:cw].astype(jnp.bfloat16)


def _layer_kernel(x_hbm, modn_ref, modm_ref, modf_ref, vec_ref, win_hbm, wa_hbm, wx_hbm, wout_hbm,
                  wgu_hbm, wd_hbm, o_hbm,
                  win_ref, wa_ref, wx_ref, wout_ref, wgu_ref, wd_ref, wstage_ref,
                  xin_ref, hmix_ref, xout_ref, x1_ref, ya_ref, u_ref, a_ref, bx_ref, mrg_ref,
                  hcar_ref, vcar_ref, rcar_ref, in_sem, out_sem, w_sem,
                  *, n_tiles, tiles_per_seq, final_norm):
    s = pl.program_id(0)
    nt = s % tiles_per_seq
    slot = s % 2
    last = n_tiles

    def load(tile):
        buf = tile % N_XIN
        return _tile_copies(x_hbm, xin_ref.at[buf], in_sem.at[buf], tile,
                            tiles_per_seq, to_hbm=False)

    def prepare(tile, mod):
        hmix_ref[tile % 2] = _rms_modulate(
            xin_ref[tile % N_XIN], vec_ref[ROW_GMIX:ROW_GMIX + 1, :],
            mod[0:1], mod[1:2]).astype(jnp.bfloat16)

    def store(tile, buf_slot):
        return _tile_copies(o_hbm, xout_ref.at[buf_slot], out_sem.at[buf_slot], tile,
                            tiles_per_seq, to_hbm=True)

    @pl.when(s == 0)
    def _():
        for cp in load(0) + load(1):
            cp.start()
        _load_weights_bf16([(win_hbm, win_ref), (wa_hbm, wa_ref), (wx_hbm, wx_ref),
                            (wout_hbm, wout_ref), (wgu_hbm, wgu_ref), (wd_hbm, wd_ref)],
                           wstage_ref, w_sem)
        for cp in load(0):
            cp.wait()
        prepare(0, modm_ref[0])
        x1_ref[1] = jnp.zeros(x1_ref.shape[1:], x1_ref.dtype)

    @pl.when(s + 2 < last)
    def _():
        for cp in load(s + 2):
            cp.start()

    @pl.when(s + 1 < last)
    def _():
        for cp in load(s + 1):
            cp.wait()

    @pl.when(s >= 3)
    def _():
        for cp in store(s - 3, slot):
            cp.wait()

    @pl.when(nt == 0)
    def _():
        hcar_ref[...] = jnp.zeros_like(hcar_ref)
        vcar_ref[...] = jnp.zeros_like(vcar_ref)
        rcar_ref[...] = jnp.zeros_like(rcar_ref)

    modm = modm_ref[0]
    modf = modf_ref[0]
    h_ffn = _rms_modulate(x1_ref[1 - slot], vec_ref[ROW_GFFN:ROW_GFFN + 1, :],
                          modf[3:4], modf[4:5]).astype(jnp.bfloat16)

    ffn_out = None
    dff = wd_ref.shape[0]
    for hd, (lo, hi) in enumerate(_ffn_chunks(dff)):
        proj = _head_proj(hd, hmix_ref[slot], win_ref)
        _mixer_head(hd, proj, nt == 0, vec_ref, wa_ref, wx_ref,
                    ya_ref, u_ref, a_ref, bx_ref, mrg_ref, hcar_ref, vcar_ref, rcar_ref)
        g_ff, u_ff = _ffn_gate_up(h_ffn, lo, hi, wgu_ref, dff)
        part = _ffn_down(g_ff, u_ff, lo, hi, wd_ref)
        ffn_out = part if ffn_out is None else ffn_out + part
        if hd == PREPARE_AFTER_HEAD:
            prepare(s + 1, modn_ref[0])
    mix_out = jnp.dot(mrg_ref[...], wout_ref[...], preferred_element_type=jnp.float32)

    x2 = x1_ref[1 - slot] + modf[5:6] * ffn_out
    if final_norm:
        ms2 = jnp.mean(x2 * x2, axis=-1, keepdims=True)
        x2 = x2 * lax.rsqrt(ms2 + EPS) * vec_ref[ROW_GFINAL:ROW_GFINAL + 1, :]
    xout_ref[slot] = x2
    x1_ref[slot] = xin_ref[s % N_XIN] + modm[2:3] * mix_out

    @pl.when(s >= 1)
    def _():
        for cp in store(s - 1, slot):
            cp.start()

    @pl.when(s == last)
    def _():
        for cp in store(s - 2, 1 - slot):
            cp.wait()
        for cp in store(s - 1, slot):
            cp.wait()


def _layer(x, mod, vecs, w_in, w_rg_a, w_rg_x, w_out, w_gate_up, w_down, final_norm):
    b, s, d = x.shape
    nts = s // TM
    n_tiles = b * nts
    assert n_tiles >= N_XIN
    const = dict(pipeline_mode=pl.Buffered(1))

    def mod_spec(shift):
        return pl.BlockSpec(
            (1, N_MOD, d), lambda i: (jnp.clip(i + shift, 0, n_tiles - 1) // nts, 0, 0))

    def whole(arr):
        return pl.BlockSpec(arr.shape, lambda i: (0,) * arr.ndim, **const)

    weights = (w_in, w_rg_a, w_rg_x, w_out, w_gate_up, w_down)
    for w in weights:
        assert w.shape[0] % W_ROWS == 0 and w.shape[1] % LANES == 0
    hbm = pl.BlockSpec(memory_space=pl.ANY)

    f32, bf16 = jnp.float32, jnp.bfloat16
    out = pl.pallas_call(
        functools.partial(_layer_kernel, n_tiles=n_tiles, tiles_per_seq=nts,
                          final_norm=final_norm),
        grid=(n_tiles + 1,),
        in_specs=[hbm, mod_spec(1), mod_spec(0), mod_spec(-1), whole(vecs)] + [hbm] * len(weights),
        out_specs=hbm,
        out_shape=jax.ShapeDtypeStruct((b, s // TIME_BLOCKS, TIME_BLOCKS, d), f32),
        scratch_shapes=[pltpu.VMEM(w.shape, bf16) for w in weights] + [
                        pltpu.VMEM((W_SLOTS, W_ROWS, W_COLS), f32),
                        pltpu.VMEM((N_XIN, TM, d), f32),
                        pltpu.VMEM((2, TM, d), bf16),
                        pltpu.VMEM((2, TM, d), f32),
                        pltpu.VMEM((2, TM, d), f32),
                        pltpu.VMEM((TM, d), f32),
                        pltpu.VMEM((TM, d), f32),
                        pltpu.VMEM((TM, d), f32),
                        pltpu.VMEM((TM, d), f32),
                        pltpu.VMEM((TM, d), bf16),
                        pltpu.VMEM((8, d), f32),
                        pltpu.VMEM((8, d), f32),
                        pltpu.VMEM((8, d), f32),
                        pltpu.SemaphoreType.DMA((N_XIN,)),
                        pltpu.SemaphoreType.DMA((2,)),
                        pltpu.SemaphoreType.DMA((W_SLOTS,))],
        compiler_params=pltpu.CompilerParams(
            dimension_semantics=("arbitrary",),
            vmem_limit_bytes=VMEM_LIMIT_BYTES),
        name="layer",
    )(x.reshape(b, s // TIME_BLOCKS, TIME_BLOCKS, d), mod, mod, mod, vecs, *weights)
    return out.reshape(b, s, d)


def kernel(x, c, w_ada, b_ada, g_norm_mix, w_in, conv_a_w, conv_b_w, conv_b_bias,
           w_rg_a, b_rg_a, w_rg_x, b_rg_x, lru_lambda, w_out, g_norm_ffn,
           w_gate_up, w_down, g_norm_final):
    depth = w_ada.shape[0]
    b, s, d = x.shape
    hw = d // RNN_HEADS
    assert s % TM == 0 and TM % (ROWS * TIME_BLOCKS) == 0 and d % LANES == 0
    assert w_in.shape[2] == N_PROJ * d and w_rg_a.shape[1:] == (RNN_HEADS, hw, hw)
    assert w_down.shape[1] % MXU_TILE == 0
    for l in range(depth):
        mod = _adaln(c, w_ada[l], b_ada[l])
        vecs = jnp.concatenate(
            [g_norm_mix[l][None], conv_a_w[l], conv_b_w[l], conv_b_bias[l][None],
             b_rg_a[l][None], b_rg_x[l][None], lru_lambda[l][None], g_norm_ffn[l][None],
             g_norm_final[None], jnp.zeros((N_VEC_ROWS - ROW_GFINAL - 1, d), jnp.float32)],
            axis=0)
        x = _layer(x, mod, vecs, w_in[l], w_rg_a[l].reshape(d, hw), w_rg_x[l].reshape(d, hw),
                   w_out[l], w_gate_up[l], w_down[l], final_norm=(l == depth - 1))
    return x
```
